```python
import math
import jax, jax.numpy as jnp
from jax import lax
import numpy as np

D_MODEL = 1024
BATCH = 32
SEQ = 2048
DEPTH = 1
DEC_BATCH = 1
DEC_SEQ = 16384
PAST_LEN = 128

MIX_WIDTH = D_MODEL
HG_WIDTH = MIX_WIDTH // 2
S5_WIDTH = MIX_WIDTH - HG_WIDTH
HG_HEADS = 4
HG_DK = 128
HG_DV = HG_WIDTH // HG_HEADS
HG_FDIM = HG_HEADS * HG_DK
HG_CHUNK = 64
S5_GROUP = 16
S5_GROUPS = S5_WIDTH // S5_GROUP
S5_STATE = 64
D_FF = 4 * D_MODEL
EPS = 1e-6
DT_MIN = 1e-3
DT_MAX = 1e-1
IN_WIDTH = 3 * HG_FDIM + 2 * HG_WIDTH + S5_WIDTH
IN_OFFSETS = (HG_FDIM, 2 * HG_FDIM, 3 * HG_FDIM, 3 * HG_FDIM + HG_WIDTH, 3 * HG_FDIM + 2 * HG_WIDTH)

kernel_name = "hymba_hgrn2_s5_bidir_encoder"


def rmsnorm(x, g):
    xf = x.astype(jnp.float32)
    y = xf * lax.rsqrt(jnp.mean(xf * xf, axis=-1, keepdims=True) + EPS) * g.astype(jnp.float32)
    return y.astype(x.dtype)


def _flip(t):
    return jnp.flip(t, axis=1)


def _chunk(t):
    b, s, h, d = t.shape
    return t.reshape(b, s // HG_CHUNK, HG_CHUNK, h, d).transpose(1, 0, 3, 2, 4)


def hgrn2_direction(q, k, lf, v):
    b, s, h, dk = q.shape
    dv = v.shape[-1]
    mask = jnp.tril(jnp.ones((HG_CHUNK, HG_CHUNK), dtype=bool))[:, :, None]

    def step(S, inp):
        qc, kc, lfc, vc = inp
        bc = jnp.cumsum(lfc, axis=-2)
        inter = jnp.einsum('bhtd,bhde->bhte', qc * jnp.exp(bc), S)
        diff = bc[:, :, :, None, :] - bc[:, :, None, :, :]
        decay = jnp.exp(jnp.where(mask, diff, -jnp.inf))
        att = jnp.einsum('bhtd,bhsd,bhtsd->bhts', qc, kc, decay)
        intra = jnp.einsum('bhts,bhse->bhte', att, vc)
        btot = bc[:, :, -1:, :]
        S_new = jnp.exp(btot[:, :, 0, :])[..., None] * S + jnp.einsum(
            'bhsd,bhse->bhde', kc * jnp.exp(btot - bc), vc)
        return S_new, inter + intra

    S0 = jnp.zeros((b, h, dk, dv), jnp.float32)
    _, o = lax.scan(step, S0, (_chunk(q), _chunk(k), _chunk(lf), _chunk(v)))
    return o.transpose(1, 0, 3, 2, 4).reshape(b, s, h, dv)


def hgrn2_mixer(q, zf_f, zf_b, vi, g, lb, onorm_g):
    b, s, _ = q.shape

    def heads(t, d):
        return t.astype(jnp.float32).reshape(b, s, HG_HEADS, d)

    qh = heads(q, HG_DK)
    vh = heads(vi, HG_DV)

    def gates(z, lbd):
        z = heads(z, HG_DK)
        lbd = lbd.reshape(HG_HEADS, HG_DK)
        f = lbd + (1.0 - lbd) * jax.nn.sigmoid(z)
        return jnp.log(f), (1.0 - lbd) * jax.nn.sigmoid(-z)

    lf_f, k_f = gates(zf_f, lb[0])
    lf_b, k_b = gates(zf_b, lb[1])
    o_f = hgrn2_direction(qh, k_f, lf_f, vh)
    o_b = _flip(hgrn2_direction(_flip(qh), _flip(k_b), _flip(lf_b), _flip(vh)))
    o = o_f + o_b
    o = o * lax.rsqrt(jnp.mean(o * o, axis=-1, keepdims=True) + EPS)
    return o.reshape(b, s, HG_WIDTH) * onorm_g.astype(jnp.float32) * jax.nn.silu(g.astype(jnp.float32))


def s5_direction(u, lam_re, lam_im, log_dt, b_re, b_im, c_re, c_im):
    s = u.shape[1]
    dt = jnp.exp(log_dt)[:, None]
    mag = jnp.exp(lam_re * dt)
    ai = lam_im * dt
    abar_re = mag * jnp.cos(ai)
    abar_im = mag * jnp.sin(ai)
    den = lam_re * lam_re + lam_im * lam_im
    nr = abar_re - 1.0
    ni = abar_im
    cr = ((nr * lam_re + ni * lam_im) / den)[..., None]
    ci = ((ni * lam_re - nr * lam_im) / den)[..., None]
    bb_re = cr * b_re - ci * b_im
    bb_im = cr * b_im + ci * b_re
    bu_re = jnp.einsum('bsgc,gpc->bsgp', u, bb_re)
    bu_im = jnp.einsum('bsgc,gpc->bsgp', u, bb_im)
    a_re = jnp.broadcast_to(abar_re, (1, s) + abar_re.shape)
    a_im = jnp.broadcast_to(abar_im, (1, s) + abar_im.shape)

    def combine(e1, e2):
        a1r, a1i, b1r, b1i = e1
        a2r, a2i, b2r, b2i = e2
        return (a1r * a2r - a1i * a2i,
                a1r * a2i + a1i * a2r,
                a2r * b1r - a2i * b1i + b2r,
                a2r * b1i + a2i * b1r + b2i)

    _, _, xr, xi = lax.associative_scan(combine, (a_re, a_im, bu_re, bu_im), axis=1)
    return jnp.einsum('bsgp,gcp->bsgc', xr, c_re) - jnp.einsum('bsgp,gcp->bsgc', xi, c_im)


def s5_mixer(u, lam_re, lam_im, log_dt, b_re, b_im, c_re, c_im, d, glu_w, glu_b, merge_g):
    b, s, _ = u.shape
    f32 = jnp.float32
    uf = u.astype(f32)
    ug = uf.reshape(b, s, S5_GROUPS, S5_GROUP)
    lam_re = lam_re.astype(f32); lam_im = lam_im.astype(f32); log_dt = log_dt.astype(f32)
    b_re = b_re.astype(f32); b_im = b_im.astype(f32); c_re = c_re.astype(f32); c_im = c_im.astype(f32)
    y_f = s5_direction(ug, lam_re[0], lam_im[0], log_dt[0], b_re[0], b_im[0], c_re[0], c_im[0])
    y_b = _flip(s5_direction(_flip(ug), lam_re[1], lam_im[1], log_dt[1], b_re[1], b_im[1], c_re[1], c_im[1]))
    y = (y_f + y_b).reshape(b, s, S5_WIDTH) + d.astype(f32) * uf
    z = jax.nn.gelu(y)
    z = z * jax.nn.sigmoid(z @ glu_w.astype(f32) + glu_b.astype(f32))
    z = z * lax.rsqrt(jnp.mean(z * z, axis=-1, keepdims=True) + EPS)
    return z * merge_g.astype(f32)


def trunk(x, norm1_g, w_in, hgrn_lb, hgrn_onorm_g, s5_lambda_re, s5_lambda_im, s5_log_dt,
          s5_b_re, s5_b_im, s5_c_re, s5_c_im, s5_d, s5_glu_w, s5_glu_b, s5_merge_g,
          w_out, norm2_g, w_ff1, w_ff2, norm_f_g):
    lbs = jnp.cumsum(jax.nn.softmax(hgrn_lb.astype(jnp.float32), axis=1), axis=1)
    for l in range(DEPTH):
        h = rmsnorm(x, norm1_g[l])
        proj = h @ w_in[l]
        q, zf_f, zf_b, vi, g, u = jnp.split(proj, IN_OFFSETS, axis=-1)
        ya = hgrn2_mixer(q, zf_f, zf_b, vi, g, lbs[:, l], hgrn_onorm_g[l])
        yb = s5_mixer(u, s5_lambda_re[l], s5_lambda_im[l], s5_log_dt[l], s5_b_re[l], s5_b_im[l],
                      s5_c_re[l], s5_c_im[l], s5_d[l], s5_glu_w[l], s5_glu_b[l], s5_merge_g[l])
        y = jnp.concatenate([ya, yb], axis=-1).astype(x.dtype) @ w_out[l]
        x = x + y
        h = rmsnorm(x, norm2_g[l])
        x = x + jnp.square(jax.nn.relu(h @ w_ff1[l])) @ w_ff2[l]
    return rmsnorm(x, norm_f_g)


def setup_inputs(seed: int = 0) -> dict:
    key = jax.random.key(seed)
    ks = jax.random.split(key, 24)
    n = jax.random.normal
    f32 = jnp.float32
    L, G, P, C = DEPTH, S5_GROUPS, S5_STATE, S5_GROUP
    return {
        "x_prompt": n(ks[0], (BATCH, SEQ, D_MODEL), f32),
        "x_sample": n(ks[1], (DEC_BATCH, DEC_SEQ, D_MODEL), f32),
        "norm1_g": 1.0 + 0.02 * n(ks[2], (L, D_MODEL), f32),
        "w_in": n(ks[3], (L, D_MODEL, IN_WIDTH), f32) * D_MODEL ** -0.5,
        "hgrn_lb": 0.5 * n(ks[4], (2, L + 1, HG_FDIM), f32),
        "hgrn_onorm_g": 1.0 + 0.02 * n(ks[5], (L, HG_WIDTH), f32),
        "s5_lambda_re": -0.5 + 0.01 * n(ks[6], (L, 2, G, P), f32),
        "s5_lambda_im": jnp.pi * jnp.arange(P, dtype=f32) + 0.01 * n(ks[7], (L, 2, G, P), f32),
        "s5_log_dt": jax.random.uniform(ks[8], (L, 2, G), f32, math.log(DT_MIN), math.log(DT_MAX)),
        "s5_b_re": n(ks[9], (L, 2, G, P, C), f32) * (1.0 / (2 * C)) ** 0.5,
        "s5_b_im": n(ks[10], (L, 2, G, P, C), f32) * (1.0 / (2 * C)) ** 0.5,
        "s5_c_re": n(ks[11], (L, 2, G, C, P), f32) * (1.0 / (2 * P)) ** 0.5,
        "s5_c_im": n(ks[12], (L, 2, G, C, P), f32) * (1.0 / (2 * P)) ** 0.5,
        "s5_d": n(ks[13], (L, S5_WIDTH), f32),
        "s5_glu_w": n(ks[14], (L, S5_WIDTH, S5_WIDTH), f32) * S5_WIDTH ** -0.5,
        "s5_glu_b": 0.01 * n(ks[15], (L, S5_WIDTH), f32),
        "s5_merge_g": 1.0 + 0.02 * n(ks[16], (L, S5_WIDTH), f32),
        "w_out": n(ks[17], (L, MIX_WIDTH, D_MODEL), f32) * MIX_WIDTH ** -0.5,
        "norm2_g": 1.0 + 0.02 * n(ks[18], (L, D_MODEL), f32),
        "w_ff1": n(ks[19], (L, D_MODEL, D_FF), f32) * D_MODEL ** -0.5,
        "w_ff2": n(ks[20], (L, D_FF, D_MODEL), f32) * D_FF ** -0.5,
        "norm_f_g": 1.0 + 0.02 * n(ks[21], (D_MODEL,), f32),
    }


def reference(x_prompt, x_sample, norm1_g, w_in, hgrn_lb, hgrn_onorm_g, s5_lambda_re, s5_lambda_im,
              s5_log_dt, s5_b_re, s5_b_im, s5_c_re, s5_c_im, s5_d, s5_glu_w, s5_glu_b, s5_merge_g,
              w_out, norm2_g, w_ff1, w_ff2, norm_f_g):
    y_prompt = trunk(x_prompt, norm1_g, w_in, hgrn_lb, hgrn_onorm_g, s5_lambda_re, s5_lambda_im,
                     s5_log_dt, s5_b_re, s5_b_im, s5_c_re, s5_c_im, s5_d, s5_glu_w, s5_glu_b,
                     s5_merge_g, w_out, norm2_g, w_ff1, w_ff2, norm_f_g)
    y_sample = trunk(x_sample, norm1_g, w_in, hgrn_lb, hgrn_onorm_g, s5_lambda_re, s5_lambda_im,
                     s5_log_dt, s5_b_re, s5_b_im, s5_c_re, s5_c_im, s5_d, s5_glu_w, s5_glu_b,
                     s5_merge_g, w_out, norm2_g, w_ff1, w_ff2, norm_f_g)
    return (y_prompt, y_sample)
```

```python
import functools

import numpy as np
import jax
import jax.numpy as jnp
from jax import lax
from jax.experimental import pallas as pl
from jax.experimental.pallas import tpu as pltpu

F32 = jnp.float32
BF16 = jnp.bfloat16

D_MODEL = 1024
HG_HEADS = 4
HG_DK = 128
HG_DV = 128
HG_FDIM = HG_HEADS * HG_DK
HG_WIDTH = HG_HEADS * HG_DV
S5_WIDTH = 512
S5_GROUP = 16
S5_GROUPS = S5_WIDTH // S5_GROUP
S5_STATE = 64
D_FF = 4 * D_MODEL
IN_WIDTH = 3 * HG_FDIM + 2 * HG_WIDTH + S5_WIDTH
EPS = 1e-6

HG_C = 64
HG_LEVELS = (32, 16, 8, 4, 2, 1)
HG_NBLK = 2 + len(HG_LEVELS)

S5_L = 16
S5_CW = S5_L * S5_GROUP
S5_SW = 4 * S5_STATE
S5_ROWS = 8

VMEM_LIMIT_BYTES = 56 * 1024 * 1024


def _cparams(sem):
    return pltpu.CompilerParams(dimension_semantics=sem, vmem_limit_bytes=VMEM_LIMIT_BYTES)


def _const_spec(shape):
    n = len(shape)
    return pl.BlockSpec(shape, lambda *_: (0,) * n, pipeline_mode=pl.Buffered(1))


def _inproj_kernel(x_ref, g1_ref, w_ref, lb_ref, q_ref, lfh_ref, lfl_ref, kk_ref, v_ref, g_ref, u_ref):
    x = x_ref[...]
    ms = jnp.mean(x * x, axis=-1, keepdims=True)
    h = (x * lax.rsqrt(ms + EPS) * g1_ref[...]).astype(BF16)
    proj = jnp.dot(h, w_ref[...], preferred_element_type=F32)
    o1 = HG_FDIM
    o2 = 3 * HG_FDIM
    o3 = o2 + HG_WIDTH
    o4 = o3 + HG_WIDTH
    q_ref[...] = proj[:, :o1].astype(BF16)
    z = proj[:, o1:o2]
    lb = lb_ref[...]
    f = lb + (1.0 - lb) * jax.nn.sigmoid(z)
    lf = jnp.log(f)
    lfh = lf.astype(BF16)
    lfh_ref[...] = lfh
    lfl_ref[...] = (lf - lfh.astype(F32)).astype(BF16)
    kk_ref[...] = ((1.0 - lb) * jax.nn.sigmoid(-z)).astype(BF16)
    v_ref[...] = proj[:, o2:o3].astype(BF16)
    g_ref[...] = proj[:, o3:o4].astype(BF16)
    u_ref[...] = proj[:, o4:].astype(BF16)


def _inproj(x, g1, w_in, lb, tm):
    t = x.shape[0]
    row = lambda w: pl.BlockSpec((tm, w), lambda i: (i, 0))
    outs = [
        jax.ShapeDtypeStruct((t, HG_FDIM), BF16),
        jax.ShapeDtypeStruct((t, 2 * HG_FDIM), BF16),
        jax.ShapeDtypeStruct((t, 2 * HG_FDIM), BF16),
        jax.ShapeDtypeStruct((t, 2 * HG_FDIM), BF16),
        jax.ShapeDtypeStruct((t, HG_WIDTH), BF16),
        jax.ShapeDtypeStruct((t, HG_WIDTH), BF16),
        jax.ShapeDtypeStruct((t, S5_WIDTH), BF16),
    ]
    return pl.pallas_call(
        _inproj_kernel,
        grid=(t // tm,),
        in_specs=[row(D_MODEL), _const_spec((1, D_MODEL)), _const_spec((D_MODEL, IN_WIDTH)),
                  _const_spec((1, 2 * HG_FDIM))],
        out_specs=[row(HG_FDIM), row(2 * HG_FDIM), row(2 * HG_FDIM), row(2 * HG_FDIM),
                   row(HG_WIDTH), row(HG_WIDTH), row(S5_WIDTH)],
        out_shape=outs,
        compiler_params=_cparams(("parallel",)),
        name="inproj",
    )(x, g1, w_in, lb)


def _hgrn_constants():
    c = HG_C
    idx = np.arange(c)
    blocks = [
        (idx[None, :] <= idx[:, None]),
        (idx[None, :] > idx[:, None]),
    ]
    masks = []
    for m in HG_LEVELS:
        rho = (idx // (2 * m)) * (2 * m) + m
        late = idx >= rho
        x = idx[None, :]
        r = idx[:, None]
        d = np.where(late[:, None], (x >= rho[:, None]) & (x <= r), (x > r) & (x < rho[:, None]))
        blocks.append(d)
        same = (idx[:, None] // (2 * m)) == (idx[None, :] // (2 * m))
        masks.append(same & late[:, None] & (~late)[None, :])
    masks.append(np.eye(c, dtype=bool))
    d_fwd = np.concatenate(blocks, axis=0).astype(np.float32)
    d_bwd = np.concatenate([b[::-1, ::-1] for b in blocks], axis=0).astype(np.float32)
    m_fwd = np.stack(masks).astype(np.float32)
    m_bwd = m_fwd[:, ::-1, ::-1]
    dall = np.stack([d_fwd, d_bwd])
    dall = np.concatenate([dall, dall], axis=-1)
    return dall, np.stack([m_fwd, m_bwd])


def _hgrn_chunk(q, k, lf2, v, st_ref, dall, mask_ref, reverse):
    c = HG_C
    nt = (((1,), (1,)), ((), ()))
    tn = (((0,), (0,)), ((), ()))
    e_all = jnp.dot(dall, lf2, preferred_element_type=F32)
    g_all = jnp.exp(e_all)
    g_cum = g_all[0:c]
    g_rest = g_all[c:2 * c]
    qf = q.astype(F32)
    kf = k.astype(F32)
    row = lax.broadcasted_iota(jnp.int32, (c, HG_DK), 0)
    if reverse:
        row = (c - 1) - row
    att = jnp.zeros((c, c), F32)
    for li, m in enumerate(HG_LEVELS):
        g_l = g_all[(2 + li) * c:(3 + li) * c]
        late = (row & (2 * m - 1)) >= m
        xl = (jnp.where(late, qf, kf) * g_l).astype(BF16)
        a_l = lax.dot_general(xl, xl, nt, preferred_element_type=F32)
        att = att + a_l * mask_ref[li]
    ones = jnp.ones((HG_DK, c), BF16)
    diag = jnp.dot((qf * kf).astype(BF16), ones, preferred_element_type=F32)
    att = att + diag * mask_ref[len(HG_LEVELS)]
    intra = jnp.dot(att.astype(BF16), v, preferred_element_type=F32)
    st = st_ref[...]
    qt = (qf * g_cum).astype(BF16)
    inter = lax.dot_general(qt, st.astype(BF16), nt, preferred_element_type=F32)
    kh = (kf * g_rest).astype(BF16)
    upd = lax.dot_general(v, kh, tn, preferred_element_type=F32)
    g_tot = g_cum[0:1] if reverse else g_cum[c - 1:c]
    st_ref[...] = st * g_tot + upd
    return inter + intra


def _hgrn_kernel(qf_ref, kf_ref, hf_ref, lf_ref, vf_ref, qb_ref, kb_ref, hb_ref, lb_ref, vb_ref,
                 dall_ref, mask_ref, of_ref, ob_ref, st_ref, *, ts):
    c = HG_C
    nch = ts // c

    @pl.when(pl.program_id(2) == 0)
    def _():
        st_ref[...] = jnp.zeros_like(st_ref)

    def body(i, carry):
        r0 = pl.multiple_of(i * c, c)
        rows = pl.ds(r0, c)
        lf2 = jnp.concatenate([hf_ref[rows, :], lf_ref[rows, :]], axis=0)
        of_ref[rows, :] = _hgrn_chunk(qf_ref[rows, :], kf_ref[rows, :], lf2, vf_ref[rows, :],
                                      st_ref.at[0], dall_ref[0], mask_ref.at[0], False)
        r1 = pl.multiple_of((nch - 1 - i) * c, c)
        rows = pl.ds(r1, c)
        lf2 = jnp.concatenate([hb_ref[rows, :], lb_ref[rows, :]], axis=0)
        ob_ref[rows, :] = _hgrn_chunk(qb_ref[rows, :], kb_ref[rows, :], lf2, vb_ref[rows, :],
                                      st_ref.at[1], dall_ref[1], mask_ref.at[1], True)
        return carry

    lax.fori_loop(0, nch, body, 0)


def _hgrn(q, lfh, lfl, kk, v, n_seq, seq, ts):
    t = q.shape[0]
    nt = seq // ts
    dall, masks = _hgrn_constants()
    dall = jnp.asarray(dall, BF16)
    masks = jnp.asarray(masks, F32)
    fwd = lambda off: pl.BlockSpec((ts, HG_DK), lambda b, h, j: (b * nt + j, h + off))
    bwd = lambda off: pl.BlockSpec((ts, HG_DK), lambda b, h, j: (b * nt + (nt - 1 - j), h + off))
    out = jax.ShapeDtypeStruct((t, HG_WIDTH), F32)
    return pl.pallas_call(
        functools.partial(_hgrn_kernel, ts=ts),
        grid=(n_seq, HG_HEADS, nt),
        in_specs=[fwd(0), fwd(0), fwd(0), fwd(0), fwd(0),
                  bwd(0), bwd(HG_HEADS), bwd(HG_HEADS), bwd(HG_HEADS), bwd(0),
                  _const_spec(dall.shape), _const_spec(masks.shape)],
        out_specs=[fwd(0), bwd(0)],
        out_shape=[out, out],
        scratch_shapes=[pltpu.VMEM((2, HG_DV, HG_DK), F32)],
        compiler_params=_cparams(("parallel", "parallel", "arbitrary")),
        name="hgrn2",
    )(q, kk, lfh, lfl, v, q, kk, lfh, lfl, v, dall, masks)


def _s5_matrices(lam_re, lam_im, log_dt, b_re, b_im, c_re, c_im):
    hp = lax.Precision.HIGHEST
    ell = S5_L
    dt = jnp.exp(log_dt)[..., None]

    def cpow(j):
        j = jnp.asarray(j, F32).reshape((-1, 1, 1, 1))
        mag = jnp.exp(lam_re * dt * j)
        ang = lam_im * dt * j
        return mag * jnp.cos(ang), mag * jnp.sin(ang)

    pr, pi = cpow(np.arange(ell + 1))
    abar_re, abar_im = pr[1], pi[1]
    den = lam_re * lam_re + lam_im * lam_im
    nr = abar_re - 1.0
    ni = abar_im
    cr = ((nr * lam_re + ni * lam_im) / den)[..., None]
    ci = ((ni * lam_re - nr * lam_im) / den)[..., None]
    bb_re = cr * b_re - ci * b_im
    bb_im = cr * b_im + ci * b_re

    wr = pr[..., None] * bb_re - pi[..., None] * bb_im
    wi = pr[..., None] * bb_im + pi[..., None] * bb_re
    kj = (jnp.einsum('dgcp,jdgpe->jdgce', c_re, wr, precision=hp)
          - jnp.einsum('dgcp,jdgpe->jdgce', c_im, wi, precision=hp))
    tau = np.arange(ell)
    lag = tau[None, :] - tau[:, None]
    kf = jnp.where((lag >= 0)[:, :, None, None, None], kj[np.clip(lag, 0, ell), 0], 0.0)
    kb = jnp.where((lag <= 0)[:, :, None, None, None], kj[np.clip(-lag, 0, ell), 1], 0.0)
    tz = (kf + kb).transpose(2, 0, 4, 1, 3)
    tz = tz.reshape(S5_GROUPS, S5_CW, S5_CW)

    mf_re, mf_im = wr[ell - 1 - tau, 0], wi[ell - 1 - tau, 0]
    mb_re, mb_im = wr[tau, 1], wi[tau, 1]
    mv = jnp.concatenate([mf_re, mb_re, mf_im, mb_im], axis=2)
    mv = mv.transpose(1, 0, 3, 2).reshape(S5_GROUPS, S5_CW, S5_SW)

    def cw(powers, d):
        p_re = pr[powers, d][:, :, None, :]
        p_im = pi[powers, d][:, :, None, :]
        return c_re[d] * p_re - c_im[d] * p_im, c_re[d] * p_im + c_im[d] * p_re

    nf_re, nf_im = cw(tau + 1, 0)
    nb_re, nb_im = cw(ell - tau, 1)
    nm = jnp.concatenate([nf_re, nb_re, -nf_im, -nb_im], axis=3)
    nm = nm.transpose(1, 3, 0, 2).reshape(S5_GROUPS, S5_SW, S5_CW)

    a16 = jnp.stack([jnp.concatenate([pr[ell, 0], pr[ell, 1]], axis=-1),
                     jnp.concatenate([pi[ell, 0], pi[ell, 1]], axis=-1)], axis=1)
    return tz.astype(BF16), mv.astype(BF16), nm.astype(BF16), a16.astype(F32)


def _s5_kernel(u_ref, tz_ref, mv_ref, nm_ref, a_ref, y_ref, v_scr, x_scr, *, nseq, nchunk, segmented):
    p = S5_STATE
    u = u_ref[0, 0]
    v_scr[...] = jnp.dot(u, mv_ref[0], preferred_element_type=F32)
    a = a_ref[0]
    ar = jnp.broadcast_to(a[0:1], (nseq, 2 * p))
    ai = jnp.broadcast_to(a[1:2], (nseq, 2 * p))
    is_fwd = lax.broadcasted_iota(jnp.int32, (nseq, 2 * p), 1) < p

    def body(i, carry):
        xr, xi = carry
        rf = pl.ds(pl.multiple_of(i * nseq, nseq), nseq)
        rb = pl.ds(pl.multiple_of((nchunk - 1 - i) * nseq, nseq), nseq)
        x_scr[rf, 0:p] = xr[:, 0:p]
        x_scr[rf, 2 * p:3 * p] = xi[:, 0:p]
        x_scr[rb, p:2 * p] = xr[:, p:2 * p]
        x_scr[rb, 3 * p:4 * p] = xi[:, p:2 * p]
        vf = v_scr[rf, :]
        vb = v_scr[rb, :]
        vr = jnp.where(is_fwd, vf[:, 0:2 * p], vb[:, 0:2 * p])
        vi = jnp.where(is_fwd, vf[:, 2 * p:4 * p], vb[:, 2 * p:4 * p])
        return ar * xr - ai * xi + vr, ar * xi + ai * xr + vi

    zero = jnp.zeros((nseq, 2 * p), F32)
    er, ei = lax.fori_loop(0, nchunk, body, (zero, zero))

    if segmented:
        sr, si = a[0:1], a[1:2]
        for _ in range(nchunk.bit_length() - 1):
            sr, si = sr * sr - si * si, 2.0 * sr * si
        z1 = jnp.zeros((1, 2 * p), F32)
        cf = [(z1, z1)]
        for s in range(nseq - 1):
            pr_, pi_ = cf[-1]
            cf.append((er[s:s + 1] + sr * pr_ - si * pi_, ei[s:s + 1] + sr * pi_ + si * pr_))
        cb = [(z1, z1)]
        for s in range(nseq - 1, 0, -1):
            pr_, pi_ = cb[0]
            cb.insert(0, (er[s:s + 1] + sr * pr_ - si * pi_, ei[s:s + 1] + sr * pi_ + si * pr_))
        cr = jnp.where(is_fwd, jnp.concatenate([t[0] for t in cf], axis=0),
                       jnp.concatenate([t[0] for t in cb], axis=0))
        ci = jnp.where(is_fwd, jnp.concatenate([t[1] for t in cf], axis=0),
                       jnp.concatenate([t[1] for t in cb], axis=0))

        def fix(i, carry):
            cr, ci = carry
            rf = pl.ds(pl.multiple_of(i * nseq, nseq), nseq)
            rb = pl.ds(pl.multiple_of((nchunk - 1 - i) * nseq, nseq), nseq)
            x_scr[rf, 0:p] += cr[:, 0:p]
            x_scr[rf, 2 * p:3 * p] += ci[:, 0:p]
            x_scr[rb, p:2 * p] += cr[:, p:2 * p]
            x_scr[rb, 3 * p:4 * p] += ci[:, p:2 * p]
            return ar * cr - ai * ci, ar * ci + ai * cr

        lax.fori_loop(0, nchunk, fix, (cr, ci))

    y = jnp.dot(u, tz_ref[0], preferred_element_type=F32)
    y = y + jnp.dot(x_scr[...].astype(BF16), nm_ref[0], preferred_element_type=F32)
    y_ref[0, 0] = y


def _s5(u, n_seq, seq, mats):
    tz, mv, nm, a16 = mats
    nseq_tile = S5_ROWS
    segmented = n_seq == 1
    if segmented:
        n_seq, seq = nseq_tile, seq // nseq_tile
    assert n_seq % nseq_tile == 0 and seq % S5_L == 0
    nchunk = seq // S5_L
    assert nchunk & (nchunk - 1) == 0
    ntile = n_seq // nseq_tile
    m = nchunk * nseq_tile
    g, c, ell = S5_GROUPS, S5_GROUP, S5_L
    ug = u.reshape(ntile, nseq_tile, nchunk, ell, g, c).transpose(4, 0, 2, 1, 3, 5)
    ug = ug.reshape(g, ntile, m, S5_CW)
    blk = pl.BlockSpec((1, 1, m, S5_CW), lambda i, j: (j, i, 0, 0))
    mat = lambda r, cdim: pl.BlockSpec((1, r, cdim), lambda i, j: (j, 0, 0))
    yg = pl.pallas_call(
        functools.partial(_s5_kernel, nseq=nseq_tile, nchunk=nchunk, segmented=segmented),
        grid=(ntile, g),
        in_specs=[blk, mat(S5_CW, S5_CW), mat(S5_CW, S5_SW), mat(S5_SW, S5_CW), mat(2, 2 * S5_STATE)],
        out_specs=blk,
        out_shape=jax.ShapeDtypeStruct((g, ntile, m, S5_CW), F32),
        scratch_shapes=[pltpu.VMEM((m, S5_SW), F32), pltpu.VMEM((m, S5_SW), F32)],
        compiler_params=_cparams(("parallel", "arbitrary")),
        name="s5",
    )(ug, tz, mv, nm, a16)
    y = yg.reshape(g, ntile, nchunk, nseq_tile, ell, c).transpose(1, 3, 2, 4, 0, 5)
    return y.reshape(n_seq * seq, S5_WIDTH)


def _rms(x, g):
    return x * lax.rsqrt(jnp.mean(x * x, axis=-1, keepdims=True) + EPS) * g


def _out_kernel(x_ref, of_ref, ob_ref, g_ref, y5_ref, u_ref, onorm_ref, d_ref, gluw_ref, glub_ref,
                merge_ref, wout_ref, n2_ref, w1_ref, w2_ref, nf_ref, out_ref):
    o = of_ref[...] + ob_ref[...]
    heads = []
    for h in range(HG_HEADS):
        oh = o[:, h * HG_DV:(h + 1) * HG_DV]
        heads.append(oh * lax.rsqrt(jnp.mean(oh * oh, axis=-1, keepdims=True) + EPS))
    g = g_ref[...].astype(F32)
    ya = jnp.concatenate(heads, axis=1) * onorm_ref[...] * (g * jax.nn.sigmoid(g))

    y = y5_ref[...] + d_ref[...] * u_ref[...].astype(F32)
    z = jax.nn.gelu(y)
    gate = jnp.dot(z.astype(BF16), gluw_ref[...], preferred_element_type=F32) + glub_ref[...]
    z = z * jax.nn.sigmoid(gate)
    yb = _rms(z, merge_ref[...])

    ycat = jnp.concatenate([ya, yb], axis=1).astype(BF16)
    x1 = x_ref[...] + jnp.dot(ycat, wout_ref[...], preferred_element_type=F32)
    h2 = _rms(x1, n2_ref[...]).astype(BF16)
    acc = x1
    step = D_FF // 4
    for c in range(4):
        hid = jnp.dot(h2, w1_ref[:, c * step:(c + 1) * step], preferred_element_type=F32)
        hid = jnp.square(jnp.maximum(hid, 0.0)).astype(BF16)
        acc = acc + jnp.dot(hid, w2_ref[c * step:(c + 1) * step, :], preferred_element_type=F32)
    out_ref[...] = _rms(acc, nf_ref[...])


def _outstage(x, o_f, o_b, g, y5, u, onorm, d, gluw, glub, merge, wout, n2, w1, w2, nf, tm):
    t = x.shape[0]
    row = lambda w: pl.BlockSpec((tm, w), lambda i: (i, 0))
    vec = lambda w: _const_spec((1, w))
    return pl.pallas_call(
        _out_kernel,
        grid=(t // tm,),
        in_specs=[row(D_MODEL), row(HG_WIDTH), row(HG_WIDTH), row(HG_WIDTH), row(S5_WIDTH), row(S5_WIDTH),
                  vec(HG_WIDTH), vec(S5_WIDTH), _const_spec((S5_WIDTH, S5_WIDTH)), vec(S5_WIDTH),
                  vec(S5_WIDTH), _const_spec((D_MODEL, D_MODEL)), vec(D_MODEL),
                  _const_spec((D_MODEL, D_FF)), _const_spec((D_FF, D_MODEL)), vec(D_MODEL)],
        out_specs=row(D_MODEL),
        out_shape=jax.ShapeDtypeStruct((t, D_MODEL), F32),
        compiler_params=_cparams(("parallel",)),
        name="outstage",
    )(x, o_f, o_b, g, y5, u, onorm, d, gluw, glub, merge, wout, n2, w1, w2, nf)


HG_TILE = 2048
ROW_TILE_IN = 512
ROW_TILE_OUT = 256


def _trunk(x3, p):
    b, s, dm = x3.shape
    x = x3.reshape(b * s, dm)
    q, lfh, lfl, kk, v, g, u = _inproj(x, p["g1"], p["w_in"], p["lb"], ROW_TILE_IN)
    o_f, o_b = _hgrn(q, lfh, lfl, kk, v, b, s, min(HG_TILE, s))
    y5 = _s5(u, b, s, p["s5"])
    out = _outstage(x, o_f, o_b, g, y5, u, p["onorm"], p["d"], p["gluw"], p["glub"], p["merge"],
                    p["wout"], p["n2"], p["w1"], p["w2"], p["nf"], ROW_TILE_OUT)
    return out.reshape(b, s, dm)


def kernel(x_prompt, x_sample, norm1_g, w_in, hgrn_lb, hgrn_onorm_g, s5_lambda_re, s5_lambda_im,
           s5_log_dt, s5_b_re, s5_b_im, s5_c_re, s5_c_im, s5_d, s5_glu_w, s5_glu_b, s5_merge_g,
           w_out, norm2_g, w_ff1, w_ff2, norm_f_g):
    l = 0
    lbs = jnp.cumsum(jax.nn.softmax(hgrn_lb.astype(F32), axis=1), axis=1)[:, l]
    row = lambda a: a.astype(F32).reshape(1, -1)
    f = lambda a: a[l].astype(F32)
    p = {
        "g1": row(norm1_g[l]),
        "w_in": w_in[l].astype(BF16),
        "lb": lbs.reshape(1, 2 * HG_FDIM),
        "s5": _s5_matrices(f(s5_lambda_re), f(s5_lambda_im), f(s5_log_dt), f(s5_b_re), f(s5_b_im),
                           f(s5_c_re), f(s5_c_im)),
        "onorm": row(hgrn_onorm_g[l]),
        "d": row(s5_d[l]),
        "gluw": s5_glu_w[l].astype(BF16),
        "glub": row(s5_glu_b[l]),
        "merge": row(s5_merge_g[l]),
        "wout": w_out[l].astype(BF16),
        "n2": row(norm2_g[l]),
        "w1": w_ff1[l].astype(BF16),
        "w2": w_ff2[l].astype(BF16),
        "nf": row(norm_f_g),
    }
    return (_trunk(x_prompt, p), _trunk(x_sample, p))
```

```python
import functools

import numpy as np
import jax
import jax.numpy as jnp
from jax import lax
from jax.experimental import pallas as pl
from jax.experimental.pallas import tpu as pltpu

F32 = jnp.float32
BF16 = jnp.bfloat16

D_MODEL = 1024
HG_HEADS = 4
HG_DK = 128
HG_DV = 128
HG_FDIM = HG_HEADS * HG_DK
HG_WIDTH = HG_HEADS * HG_DV
S5_WIDTH = 512
S5_GROUP = 16
S5_GROUPS = S5_WIDTH // S5_GROUP
S5_STATE = 64
D_FF = 4 * D_MODEL
IN_WIDTH = 3 * HG_FDIM + 2 * HG_WIDTH + S5_WIDTH
EPS = 1e-6

LANES = 128
VMEM_LIMIT_BYTES = 56 * 1024 * 1024

HG_C = 64
HG_LEVELS = (32, 16, 8, 4, 2, 1)
HG_NBLK = 2 + len(HG_LEVELS)
HG_NC = 2

S5_L = 16
S5_OCTETS = S5_WIDTH // LANES
S5_GPO = LANES // S5_GROUP
S5_OW = S5_L * LANES
S5_SW = 4 * S5_STATE * S5_GPO
S5_ROWS = 8


def _cparams(sem):
    return pltpu.CompilerParams(dimension_semantics=sem, vmem_limit_bytes=VMEM_LIMIT_BYTES)


def _const_spec(shape):
    n = len(shape)
    return pl.BlockSpec(shape, lambda *_: (0,) * n, pipeline_mode=pl.Buffered(1))


def _inproj_kernel(x_ref, g1_ref, w_ref, lb_ref, q_ref, lfh_ref, lfl_ref, kk_ref, v_ref, g_ref, u_ref,
                   u_scr):
    x = x_ref[...]
    ms = jnp.mean(x * x, axis=-1, keepdims=True)
    h = (x * lax.rsqrt(ms + EPS) * g1_ref[...]).astype(BF16)
    proj = jnp.dot(h, w_ref[...], preferred_element_type=F32)
    o1 = HG_FDIM
    o2 = 3 * HG_FDIM
    o3 = o2 + HG_WIDTH
    o4 = o3 + HG_WIDTH
    q_ref[...] = proj[:, :o1].astype(BF16)
    z = proj[:, o1:o2]
    lb = lb_ref[...]
    f = lb + (1.0 - lb) * jax.nn.sigmoid(z)
    lf = jnp.log2(f)
    lfh = lf.astype(BF16)
    lfh_ref[...] = lfh
    lfl_ref[...] = (lf - lfh.astype(F32)).astype(BF16)
    kk_ref[...] = ((1.0 - lb) * jax.nn.sigmoid(-z)).astype(BF16)
    v_ref[...] = proj[:, o2:o3].astype(BF16)
    g_ref[...] = proj[:, o3:o4].astype(BF16)
    nrow = u_scr.shape[1] // S5_L
    for o in range(S5_OCTETS):
        u_scr[o] = proj[:, o4 + o * LANES:o4 + (o + 1) * LANES]
        for tau in range(S5_L):
            piece = u_scr[o, pl.ds(tau, nrow, stride=S5_L), :]
            u_ref[:, (o * S5_L + tau) * LANES:(o * S5_L + tau + 1) * LANES] = piece.astype(BF16)


def _inproj(x, g1, w_in, lb, tm):
    t = x.shape[0]
    row = lambda w: pl.BlockSpec((tm, w), lambda i: (i, 0))
    outs = [
        jax.ShapeDtypeStruct((t, HG_FDIM), BF16),
        jax.ShapeDtypeStruct((t, 2 * HG_FDIM), BF16),
        jax.ShapeDtypeStruct((t, 2 * HG_FDIM), BF16),
        jax.ShapeDtypeStruct((t, 2 * HG_FDIM), BF16),
        jax.ShapeDtypeStruct((t, HG_WIDTH), BF16),
        jax.ShapeDtypeStruct((t, HG_WIDTH), BF16),
        jax.ShapeDtypeStruct((t // S5_L, S5_OCTETS * S5_OW), BF16),
    ]
    return pl.pallas_call(
        _inproj_kernel,
        grid=(t // tm,),
        in_specs=[row(D_MODEL), _const_spec((1, D_MODEL)), _const_spec((D_MODEL, IN_WIDTH)),
                  _const_spec((1, 2 * HG_FDIM))],
        out_specs=[row(HG_FDIM), row(2 * HG_FDIM), row(2 * HG_FDIM), row(2 * HG_FDIM),
                   row(HG_WIDTH), row(HG_WIDTH),
                   pl.BlockSpec((tm // S5_L, S5_OCTETS * S5_OW), lambda i: (i, 0))],
        out_shape=outs,
        scratch_shapes=[pltpu.VMEM((S5_OCTETS, tm, LANES), F32)],
        compiler_params=_cparams(("parallel",)),
        name="inproj",
    )(x, g1, w_in, lb)


def _hgrn_constants():
    c = HG_C
    idx = np.arange(c)
    blocks = [
        (idx[None, :] <= idx[:, None]),
        (idx[None, :] > idx[:, None]),
    ]
    masks = []
    for m in HG_LEVELS:
        rho = (idx // (2 * m)) * (2 * m) + m
        late = idx >= rho
        x = idx[None, :]
        r = idx[:, None]
        d = np.where(late[:, None], (x >= rho[:, None]) & (x <= r), (x > r) & (x < rho[:, None]))
        blocks.append(d)
        same = (idx[:, None] // (2 * m)) == (idx[None, :] // (2 * m))
        masks.append(same & late[:, None] & (~late)[None, :])
    masks.append(np.eye(c, dtype=bool))
    d_fwd = np.concatenate(blocks, axis=0).astype(np.float32)
    d_bwd = np.concatenate([b[::-1, ::-1] for b in blocks], axis=0).astype(np.float32)
    m_fwd = np.stack(masks).astype(np.float32)
    m_bwd = m_fwd[:, ::-1, ::-1]
    dall = np.stack([d_fwd, d_bwd])
    dall = np.concatenate([dall, dall], axis=-1)
    return dall, np.stack([m_fwd, m_bwd])


def _hgrn_group(q, k, lfh, lfl, v, st_ref, dall, mask_ref, reverse):
    c, nc = HG_C, HG_NC
    nt = (((1,), (1,)), ((), ()))
    tn = (((0,), (0,)), ((), ()))
    sl = lambda a, j: a[j * c:(j + 1) * c]
    lf2 = jnp.concatenate(
        [jnp.concatenate([sl(lfh, j), sl(lfl, j)], axis=0) for j in range(nc)], axis=1)
    e_all = jnp.dot(dall, lf2, preferred_element_type=F32)
    g_all = jnp.exp2(e_all)
    row = lax.broadcasted_iota(jnp.int32, (c, HG_DK), 0)
    if reverse:
        row = (c - 1) - row
    ones = jnp.ones((HG_DK, c), BF16)
    parts = []
    for j in range(nc):
        gj = g_all[:, j * HG_DK:(j + 1) * HG_DK]
        qf = sl(q, j).astype(F32)
        kf = sl(k, j).astype(F32)
        vj = sl(v, j)
        att = jnp.zeros((c, c), F32)
        for li, m in enumerate(HG_LEVELS):
            g_l = gj[(2 + li) * c:(3 + li) * c]
            if m >= 8:
                late_blk = lambda b: ((c // m - 1 - b) if reverse else b) % 2 == 1
                qk = jnp.concatenate(
                    [(qf if late_blk(b) else kf)[b * m:(b + 1) * m] for b in range(c // m)], axis=0)
            else:
                qk = jnp.where((row & (2 * m - 1)) >= m, qf, kf)
            xl = (qk * g_l).astype(BF16)
            a_l = lax.dot_general(xl, xl, nt, preferred_element_type=F32)
            att = att + a_l * mask_ref[li]
        diag = jnp.dot((qf * kf).astype(BF16), ones, preferred_element_type=F32)
        att = att + diag * mask_ref[len(HG_LEVELS)]
        intra = jnp.dot(att.astype(BF16), vj, preferred_element_type=F32)
        g_cum = gj[0:c]
        qt = (qf * g_cum).astype(BF16)
        kh = (kf * gj[c:2 * c]).astype(BF16)
        upd = lax.dot_general(vj, kh, tn, preferred_element_type=F32)
        g_tot = g_cum[0:1] if reverse else g_cum[c - 1:c]
        parts.append((intra, qt, g_tot, upd))
    st = st_ref[...]
    outs = [None] * nc
    for j in (range(nc - 1, -1, -1) if reverse else range(nc)):
        intra, qt, g_tot, upd = parts[j]
        outs[j] = intra + lax.dot_general(qt, st.astype(BF16), nt, preferred_element_type=F32)
        st = st * g_tot + upd
    st_ref[...] = st
    return jnp.concatenate(outs, axis=0)


def _hgrn_kernel(qf_ref, kf_ref, hf_ref, lf_ref, vf_ref, qb_ref, kb_ref, hb_ref, lb_ref, vb_ref,
                 dall_ref, mask_ref, of_ref, ob_ref, st_ref, *, ts):
    gr = HG_C * HG_NC
    ngr = ts // gr

    @pl.when(pl.program_id(2) == 0)
    def _():
        st_ref[...] = jnp.zeros_like(st_ref)

    def body(i, carry):
        rows = pl.ds(pl.multiple_of(i * gr, gr), gr)
        of_ref[rows, :] = _hgrn_group(qf_ref[rows, :], kf_ref[rows, :], hf_ref[rows, :], lf_ref[rows, :],
                                      vf_ref[rows, :], st_ref.at[0], dall_ref[0], mask_ref.at[0], False)
        rows = pl.ds(pl.multiple_of((ngr - 1 - i) * gr, gr), gr)
        ob_ref[rows, :] = _hgrn_group(qb_ref[rows, :], kb_ref[rows, :], hb_ref[rows, :], lb_ref[rows, :],
                                      vb_ref[rows, :], st_ref.at[1], dall_ref[1], mask_ref.at[1], True)
        return carry

    lax.fori_loop(0, ngr, body, 0)


def _hgrn(q, lfh, lfl, kk, v, n_seq, seq, ts):
    t = q.shape[0]
    nt = seq // ts
    dall, masks = _hgrn_constants()
    dall = jnp.asarray(dall, BF16)
    masks = jnp.asarray(masks, F32)
    fwd = lambda off: pl.BlockSpec((ts, HG_DK), lambda b, h, j: (b * nt + j, h + off))
    bwd = lambda off: pl.BlockSpec((ts, HG_DK), lambda b, h, j: (b * nt + (nt - 1 - j), h + off))
    out = jax.ShapeDtypeStruct((t, HG_WIDTH), F32)
    return pl.pallas_call(
        functools.partial(_hgrn_kernel, ts=ts),
        grid=(n_seq, HG_HEADS, nt),
        in_specs=[fwd(0), fwd(0), fwd(0), fwd(0), fwd(0),
                  bwd(0), bwd(HG_HEADS), bwd(HG_HEADS), bwd(HG_HEADS), bwd(0),
                  _const_spec(dall.shape), _const_spec(masks.shape)],
        out_specs=[fwd(0), bwd(0)],
        out_shape=[out, out],
        scratch_shapes=[pltpu.VMEM((2, HG_DV, HG_DK), F32)],
        compiler_params=_cparams(("parallel", "parallel", "arbitrary")),
        name="hgrn2",
    )(q, kk, lfh, lfl, v, q, kk, lfh, lfl, v, dall, masks)


def _s5_matrices(lam_re, lam_im, log_dt, b_re, b_im, c_re, c_im, d_skip):
    hp = lax.Precision.HIGHEST
    ell, g, c, p = S5_L, S5_GROUPS, S5_GROUP, S5_STATE
    no, gpo = S5_OCTETS, S5_GPO
    dt = jnp.exp(log_dt)[..., None]

    def cpow(j):
        j = jnp.asarray(j, F32).reshape((-1, 1, 1, 1))
        mag = jnp.exp(lam_re * dt * j)
        ang = lam_im * dt * j
        return mag * jnp.cos(ang), mag * jnp.sin(ang)

    pr, pi = cpow(np.arange(ell + 1))
    abar_re, abar_im = pr[1], pi[1]
    den = lam_re * lam_re + lam_im * lam_im
    nr = abar_re - 1.0
    ni = abar_im
    cr = ((nr * lam_re + ni * lam_im) / den)[..., None]
    ci = ((ni * lam_re - nr * lam_im) / den)[..., None]
    bb_re = cr * b_re - ci * b_im
    bb_im = cr * b_im + ci * b_re

    wr = pr[..., None] * bb_re - pi[..., None] * bb_im
    wi = pr[..., None] * bb_im + pi[..., None] * bb_re
    kj = (jnp.einsum('dgcp,jdgpe->jdgce', c_re, wr, precision=hp)
          - jnp.einsum('dgcp,jdgpe->jdgce', c_im, wi, precision=hp))
    tau = np.arange(ell)
    lag = tau[None, :] - tau[:, None]
    kf = jnp.where((lag >= 0)[:, :, None, None, None], kj[np.clip(lag, 0, ell), 0], 0.0)
    kb = jnp.where((lag <= 0)[:, :, None, None, None], kj[np.clip(-lag, 0, ell), 1], 0.0)
    skip = (jnp.eye(ell)[:, :, None, None, None] * jnp.eye(c)[None, None, None]
            * d_skip.reshape(1, 1, g, c, 1))
    tz = (kf + kb + skip).transpose(2, 0, 4, 1, 3)

    mf_re, mf_im = wr[ell - 1 - tau, 0], wi[ell - 1 - tau, 0]
    mb_re, mb_im = wr[tau, 1], wi[tau, 1]
    mv = jnp.stack([jnp.stack([mf_re, mb_re], axis=-2), jnp.stack([mf_im, mb_im], axis=-2)], axis=-3)
    mv = mv.transpose(1, 0, 5, 3, 4, 2)

    def cw(powers, d):
        p_re = pr[powers, d][:, :, None, :]
        p_im = pi[powers, d][:, :, None, :]
        return c_re[d] * p_re - c_im[d] * p_im, c_re[d] * p_im + c_im[d] * p_re

    nf_re, nf_im = cw(tau + 1, 0)
    nb_re, nb_im = cw(ell - tau, 1)
    nm = jnp.stack([jnp.stack([nf_re, nb_re], axis=-2), jnp.stack([-nf_im, -nb_im], axis=-2)], axis=-3)
    nm = nm.transpose(1, 3, 4, 5, 0, 2)

    eye = jnp.eye(gpo, dtype=F32)
    tz = tz.reshape(no, gpo, ell, c, ell, c)
    tz = jnp.einsum('oipcqd,ij->opicqjd', tz, eye).reshape(no, S5_OW, S5_OW)
    mv = mv.reshape(no, gpo, ell, c, 2, 2 * p)
    mv = jnp.einsum('oipcrs,ij->opicrjs', mv, eye).reshape(no, S5_OW, S5_SW)
    nm = nm.reshape(no, gpo, 2, 2 * p, ell, c)
    nm = jnp.einsum('oirsqd,ij->orisqjd', nm, eye).reshape(no, S5_SW, S5_OW)

    a_re = jnp.concatenate([pr[ell, 0], pr[ell, 1]], axis=-1).reshape(no, gpo * 2 * p)
    a_im = jnp.concatenate([pi[ell, 0], pi[ell, 1]], axis=-1).reshape(no, gpo * 2 * p)
    a16 = jnp.stack([a_re, a_im], axis=1)
    return tz.astype(BF16), mv.astype(BF16), nm.astype(BF16), a16.astype(F32)


def _s5_state_kernel(u_ref, mv_ref, a_ref, x_ref, v_scr, *, nseq, nchunk, segmented):
    hw = S5_SW // 2
    ntile = S5_SW // LANES
    tiles = lambda val: [val[:, k * LANES:(k + 1) * LANES] for k in range(ntile)]

    def ld(r):
        return jnp.concatenate([v_scr[k, r, :] for k in range(ntile)], axis=1)

    def st(r, val):
        for k, piece in enumerate(tiles(val)):
            v_scr[k, r, :] = piece

    for s in range(nseq):
        st(pl.ds(s, nchunk, stride=nseq),
           jnp.dot(u_ref[s * nchunk:(s + 1) * nchunk, :], mv_ref[0], preferred_element_type=F32))
    a = a_ref[0]
    ar = jnp.broadcast_to(a[0:1], (nseq, hw))
    ai = jnp.broadcast_to(a[1:2], (nseq, hw))
    lane = lax.broadcasted_iota(jnp.int32, (nseq, S5_SW), 1)
    is_fwd2 = (lane & (2 * S5_STATE - 1)) < S5_STATE
    is_fwd = is_fwd2[:, :hw]

    def rows(i):
        rf = pl.ds(pl.multiple_of(i * nseq, nseq), nseq)
        rb = pl.ds(pl.multiple_of((nchunk - 1 - i) * nseq, nseq), nseq)
        return rf, rb

    def body(i, carry):
        xr, xi = carry
        rf, rb = rows(i)
        vf = ld(rf)
        vb = ld(rb)
        x = jnp.concatenate([xr, xi], axis=1)
        st(rf, jnp.where(is_fwd2, x, vf))
        st(rb, jnp.where(is_fwd2, vb, x))
        vr = jnp.where(is_fwd, vf[:, :hw], vb[:, :hw])
        vi = jnp.where(is_fwd, vf[:, hw:], vb[:, hw:])
        return ar * xr - ai * xi + vr, ar * xi + ai * xr + vi

    zero = jnp.zeros((nseq, hw), F32)
    er, ei = lax.fori_loop(0, nchunk, body, (zero, zero))

    if segmented:
        sr, si = a[0:1], a[1:2]
        for _ in range(nchunk.bit_length() - 1):
            sr, si = sr * sr - si * si, 2.0 * sr * si
        z1 = jnp.zeros((1, hw), F32)
        cf = [(z1, z1)]
        for s in range(nseq - 1):
            pr_, pi_ = cf[-1]
            cf.append((er[s:s + 1] + sr * pr_ - si * pi_, ei[s:s + 1] + sr * pi_ + si * pr_))
        cb = [(z1, z1)]
        for s in range(nseq - 1, 0, -1):
            pr_, pi_ = cb[0]
            cb.insert(0, (er[s:s + 1] + sr * pr_ - si * pi_, ei[s:s + 1] + sr * pi_ + si * pr_))
        cr = jnp.where(is_fwd, jnp.concatenate([t[0] for t in cf], axis=0),
                       jnp.concatenate([t[0] for t in cb], axis=0))
        ci = jnp.where(is_fwd, jnp.concatenate([t[1] for t in cf], axis=0),
                       jnp.concatenate([t[1] for t in cb], axis=0))

        def fix(i, carry):
            cr, ci = carry
            rf, rb = rows(i)
            x = jnp.concatenate([cr, ci], axis=1)
            st(rf, ld(rf) + jnp.where(is_fwd2, x, 0.0))
            st(rb, ld(rb) + jnp.where(is_fwd2, 0.0, x))
            return ar * cr - ai * ci, ar * ci + ai * cr

        lax.fori_loop(0, nchunk, fix, (cr, ci))

    for s in range(nseq):
        x_ref[s * nchunk:(s + 1) * nchunk, :] = ld(pl.ds(s, nchunk, stride=nseq)).astype(BF16)


def _s5_out_kernel(u_ref, x_ref, tz_ref, nm_ref, y_ref):
    y = jnp.dot(u_ref[...], tz_ref[0], preferred_element_type=F32)
    y = y + jnp.dot(x_ref[...], nm_ref[0], preferred_element_type=F32)
    y_ref[...] = y.astype(BF16)


def _s5(u2, n_seq, seq, mats, tm_out):
    tz, mv, nm, a16 = mats
    nrow = u2.shape[0]
    nseq = S5_ROWS
    segmented = n_seq == 1
    if segmented:
        n_seq, seq = nseq, seq // nseq
    assert n_seq % nseq == 0 and seq % S5_L == 0
    nchunk = seq // S5_L
    assert nchunk & (nchunk - 1) == 0 and nchunk % 16 == 0
    m = nchunk * nseq
    blk = lambda r: pl.BlockSpec((r, S5_OW), lambda o, i: (i, o))
    mat = lambda r, c: pl.BlockSpec((1, r, c), lambda o, i: (o, 0, 0), pipeline_mode=pl.Buffered(1))
    xin = pl.pallas_call(
        functools.partial(_s5_state_kernel, nseq=nseq, nchunk=nchunk, segmented=segmented),
        grid=(S5_OCTETS, nrow // m),
        in_specs=[blk(m), mat(S5_OW, S5_SW), mat(2, S5_SW // 2)],
        out_specs=blk(m),
        out_shape=jax.ShapeDtypeStruct((nrow, S5_OCTETS * S5_SW), BF16),
        scratch_shapes=[pltpu.VMEM((S5_SW // LANES, m, LANES), F32)],
        compiler_params=_cparams(("parallel", "arbitrary")),
        name="s5_state",
    )(u2, mv, a16)
    return pl.pallas_call(
        _s5_out_kernel,
        grid=(S5_OCTETS, nrow // tm_out),
        in_specs=[blk(tm_out), blk(tm_out), mat(S5_OW, S5_OW), mat(S5_SW, S5_OW)],
        out_specs=blk(tm_out),
        out_shape=jax.ShapeDtypeStruct((nrow, S5_OCTETS * S5_OW), BF16),
        compiler_params=_cparams(("parallel", "arbitrary")),
        name="s5_out",
    )(u2, xin, tz, nm)


def _rms(x, g):
    return x * lax.rsqrt(jnp.mean(x * x, axis=-1, keepdims=True) + EPS) * g


def _out_kernel(x_ref, of_ref, ob_ref, g_ref, y2_ref, onorm_ref, gluw_ref, glub_ref,
                merge_ref, wout_ref, n2_ref, w1_ref, w2_ref, nf_ref, out_ref, y_scr):
    o = of_ref[...] + ob_ref[...]
    heads = []
    for h in range(HG_HEADS):
        oh = o[:, h * HG_DV:(h + 1) * HG_DV]
        heads.append(oh * lax.rsqrt(jnp.mean(oh * oh, axis=-1, keepdims=True) + EPS))
    g = g_ref[...].astype(F32)
    ya = jnp.concatenate(heads, axis=1) * onorm_ref[...] * (g * jax.nn.sigmoid(g))

    nrow = y_scr.shape[1] // S5_L
    for oc in range(S5_OCTETS):
        for tau in range(S5_L):
            piece = y2_ref[:, (oc * S5_L + tau) * LANES:(oc * S5_L + tau + 1) * LANES]
            y_scr[oc, pl.ds(tau, nrow, stride=S5_L), :] = piece.astype(F32)
    z = jax.nn.gelu(jnp.concatenate([y_scr[oc] for oc in range(S5_OCTETS)], axis=1))
    gate = jnp.dot(z.astype(BF16), gluw_ref[...], preferred_element_type=F32) + glub_ref[...]
    z = z * jax.nn.sigmoid(gate)
    yb = _rms(z, merge_ref[...])

    ycat = jnp.concatenate([ya, yb], axis=1).astype(BF16)
    x1 = x_ref[...] + jnp.dot(ycat, wout_ref[...], preferred_element_type=F32)
    h2 = _rms(x1, n2_ref[...]).astype(BF16)
    acc = x1
    step = D_FF // 4
    for c in range(4):
        hid = jnp.dot(h2, w1_ref[:, c * step:(c + 1) * step], preferred_element_type=F32)
        hid = jnp.square(jnp.maximum(hid, 0.0)).astype(BF16)
        acc = acc + jnp.dot(hid, w2_ref[c * step:(c + 1) * step, :], preferred_element_type=F32)
    out_ref[...] = _rms(acc, nf_ref[...])


def _outstage(x, o_f, o_b, g, y2, onorm, gluw, glub, merge, wout, n2, w1, w2, nf, tm):
    t = x.shape[0]
    row = lambda w: pl.BlockSpec((tm, w), lambda i: (i, 0))
    vec = lambda w: _const_spec((1, w))
    return pl.pallas_call(
        _out_kernel,
        grid=(t // tm,),
        in_specs=[row(D_MODEL), row(HG_WIDTH), row(HG_WIDTH), row(HG_WIDTH),
                  pl.BlockSpec((tm // S5_L, S5_OCTETS * S5_OW), lambda i: (i, 0)),
                  vec(HG_WIDTH), _const_spec((S5_WIDTH, S5_WIDTH)), vec(S5_WIDTH),
                  vec(S5_WIDTH), _const_spec((D_MODEL, D_MODEL)), vec(D_MODEL),
                  _const_spec((D_MODEL, D_FF)), _const_spec((D_FF, D_MODEL)), vec(D_MODEL)],
        out_specs=row(D_MODEL),
        out_shape=jax.ShapeDtypeStruct((t, D_MODEL), F32),
        scratch_shapes=[pltpu.VMEM((S5_OCTETS, tm, LANES), F32)],
        compiler_params=_cparams(("parallel",)),
        name="outstage",
    )(x, o_f, o_b, g, y2, onorm, gluw, glub, merge, wout, n2, w1, w2, nf)


HG_TILE = 2048
ROW_TILE_IN = 512
ROW_TILE_OUT = 256
S5_TILE_OUT = 512


def _trunk(x3, p):
    b, s, dm = x3.shape
    x = x3.reshape(b * s, dm)
    q, lfh, lfl, kk, v, g, u2 = _inproj(x, p["g1"], p["w_in"], p["lb"], ROW_TILE_IN)
    o_f, o_b = _hgrn(q, lfh, lfl, kk, v, b, s, min(HG_TILE, s))
    y2 = _s5(u2, b, s, p["s5"], min(S5_TILE_OUT, b * s // S5_L))
    out = _outstage(x, o_f, o_b, g, y2, p["onorm"], p["gluw"], p["glub"], p["merge"],
                    p["wout"], p["n2"], p["w1"], p["w2"], p["nf"], ROW_TILE_OUT)
    return out.reshape(b, s, dm)


def kernel(x_prompt, x_sample, norm1_g, w_in, hgrn_lb, hgrn_onorm_g, s5_lambda_re, s5_lambda_im,
           s5_log_dt, s5_b_re, s5_b_im, s5_c_re, s5_c_im, s5_d, s5_glu_w, s5_glu_b, s5_merge_g,
           w_out, norm2_g, w_ff1, w_ff2, norm_f_g):
    l = 0
    lbs = jnp.cumsum(jax.nn.softmax(hgrn_lb.astype(F32), axis=1), axis=1)[:, l]
    row = lambda a: a.astype(F32).reshape(1, -1)
    f = lambda a: a[l].astype(F32)
    p = {
        "g1": row(norm1_g[l]),
        "w_in": w_in[l].astype(BF16),
        "lb": lbs.reshape(1, 2 * HG_FDIM),
        "s5": _s5_matrices(f(s5_lambda_re), f(s5_lambda_im), f(s5_log_dt), f(s5_b_re), f(s5_b_im),
                           f(s5_c_re), f(s5_c_im), f(s5_d)),
        "onorm": row(hgrn_onorm_g[l]),
        "gluw": s5_glu_w[l].astype(BF16),
        "glub": row(s5_glu_b[l]),
        "merge": row(s5_merge_g[l]),
        "wout": w_out[l].astype(BF16),
        "n2": row(norm2_g[l]),
        "w1": w_ff1[l].astype(BF16),
        "w2": w_ff2[l].astype(BF16),
        "nf": row(norm_f_g),
    }
    return (_trunk(x_prompt, p), _trunk(x_sample, p))
```

```python
import functools

import numpy as np
import jax
import jax.numpy as jnp
from jax import lax
from jax.experimental import pallas as pl
from jax.experimental.pallas import tpu as pltpu

F32 = jnp.float32
BF16 = jnp.bfloat16

D_MODEL = 1024
HG_HEADS = 4
HG_DK = 128
HG_DV = 128
HG_FDIM = HG_HEADS * HG_DK
HG_WIDTH = HG_HEADS * HG_DV
S5_WIDTH = 512
S5_GROUP = 16
S5_GROUPS = S5_WIDTH // S5_GROUP
S5_STATE = 64
D_FF = 4 * D_MODEL
IN_WIDTH = 3 * HG_FDIM + 2 * HG_WIDTH + S5_WIDTH
EPS = 1e-6

LANES = 128
VMEM_LIMIT_BYTES = 56 * 1024 * 1024

HG_C = 64
HG_LEVELS = (32, 16, 8, 4, 2, 1)
HG_NBLK = 2 + len(HG_LEVELS)
HG_NC = 4

S5_L = 16
S5_OCTETS = S5_WIDTH // LANES
S5_GPO = LANES // S5_GROUP
S5_OW = S5_L * LANES
S5_SW = 4 * S5_STATE * S5_GPO
S5_ROWS = 8


def _cparams(sem):
    return pltpu.CompilerParams(dimension_semantics=sem, vmem_limit_bytes=VMEM_LIMIT_BYTES)


def _const_spec(shape):
    n = len(shape)
    return pl.BlockSpec(shape, lambda *_: (0,) * n, pipeline_mode=pl.Buffered(1))


def _inproj_kernel(x_ref, g1_ref, w_ref, lb_ref, q_ref, lfh_ref, lfl_ref, kk_ref, v_ref, g_ref, u_ref,
                   u_scr):
    x = x_ref[...]
    ms = jnp.mean(x * x, axis=-1, keepdims=True)
    h = (x * lax.rsqrt(ms + EPS) * g1_ref[...]).astype(BF16)
    proj = jnp.dot(h, w_ref[...], preferred_element_type=F32)
    o1 = HG_FDIM
    o2 = 3 * HG_FDIM
    o3 = o2 + HG_WIDTH
    o4 = o3 + HG_WIDTH
    q_ref[...] = proj[:, :o1].astype(BF16)
    z = proj[:, o1:o2]
    lb = lb_ref[...]
    f = lb + (1.0 - lb) * jax.nn.sigmoid(z)
    lf = jnp.log2(f)
    lfh = lf.astype(BF16)
    lfh_ref[...] = lfh
    lfl_ref[...] = (lf - lfh.astype(F32)).astype(BF16)
    kk_ref[...] = ((1.0 - lb) * jax.nn.sigmoid(-z)).astype(BF16)
    v_ref[...] = proj[:, o2:o3].astype(BF16)
    g_ref[...] = proj[:, o3:o4].astype(BF16)
    nrow = u_scr.shape[1] // S5_L
    for o in range(S5_OCTETS):
        u_scr[o] = proj[:, o4 + o * LANES:o4 + (o + 1) * LANES]
        for tau in range(S5_L):
            piece = u_scr[o, pl.ds(tau, nrow, stride=S5_L), :]
            u_ref[:, (o * S5_L + tau) * LANES:(o * S5_L + tau + 1) * LANES] = piece.astype(BF16)


def _inproj(x, g1, w_in, lb, tm):
    t = x.shape[0]
    row = lambda w: pl.BlockSpec((tm, w), lambda i: (i, 0))
    outs = [
        jax.ShapeDtypeStruct((t, HG_FDIM), BF16),
        jax.ShapeDtypeStruct((t, 2 * HG_FDIM), BF16),
        jax.ShapeDtypeStruct((t, 2 * HG_FDIM), BF16),
        jax.ShapeDtypeStruct((t, 2 * HG_FDIM), BF16),
        jax.ShapeDtypeStruct((t, HG_WIDTH), BF16),
        jax.ShapeDtypeStruct((t, HG_WIDTH), BF16),
        jax.ShapeDtypeStruct((t // S5_L, S5_OCTETS * S5_OW), BF16),
    ]
    return pl.pallas_call(
        _inproj_kernel,
        grid=(t // tm,),
        in_specs=[row(D_MODEL), _const_spec((1, D_MODEL)), _const_spec((D_MODEL, IN_WIDTH)),
                  _const_spec((1, 2 * HG_FDIM))],
        out_specs=[row(HG_FDIM), row(2 * HG_FDIM), row(2 * HG_FDIM), row(2 * HG_FDIM),
                   row(HG_WIDTH), row(HG_WIDTH),
                   pl.BlockSpec((tm // S5_L, S5_OCTETS * S5_OW), lambda i: (i, 0))],
        out_shape=outs,
        scratch_shapes=[pltpu.VMEM((S5_OCTETS, tm, LANES), F32)],
        compiler_params=_cparams(("parallel",)),
        name="inproj",
    )(x, g1, w_in, lb)


def _hgrn_constants():
    c = HG_C
    idx = np.arange(c)
    blocks = [
        (idx[None, :] <= idx[:, None]),
        (idx[None, :] > idx[:, None]),
    ]
    masks = []
    for m in HG_LEVELS:
        rho = (idx // (2 * m)) * (2 * m) + m
        late = idx >= rho
        x = idx[None, :]
        r = idx[:, None]
        d = np.where(late[:, None], (x >= rho[:, None]) & (x <= r), (x > r) & (x < rho[:, None]))
        blocks.append(d)
        same = (idx[:, None] // (2 * m)) == (idx[None, :] // (2 * m))
        masks.append(same & late[:, None] & (~late)[None, :])
    masks.append(np.eye(c, dtype=bool))
    d_fwd = np.concatenate(blocks, axis=0).astype(np.float32)
    d_bwd = np.concatenate([b[::-1, ::-1] for b in blocks], axis=0).astype(np.float32)
    m_fwd = np.stack(masks).astype(np.float32)
    m_bwd = m_fwd[:, ::-1, ::-1]
    dall = np.stack([d_fwd, d_bwd])
    dall = np.concatenate([dall, dall], axis=-1)
    return dall, np.stack([m_fwd, m_bwd])


def _hgrn_group(q, k, lfh, lfl, v, st_ref, dall, mask_ref, reverse):
    c, nc = HG_C, HG_NC
    nt = (((1,), (1,)), ((), ()))
    tn = (((0,), (0,)), ((), ()))
    sl = lambda a, j: a[j * c:(j + 1) * c]
    lf2 = jnp.concatenate(
        [jnp.concatenate([sl(lfh, j), sl(lfl, j)], axis=0) for j in range(nc)], axis=1)
    e_all = jnp.dot(dall, lf2, preferred_element_type=F32)
    g_all = jnp.exp2(e_all)
    row = lax.broadcasted_iota(jnp.int32, (c, HG_DK), 0)
    if reverse:
        row = (c - 1) - row
    ones = jnp.ones((HG_DK, c), BF16)
    parts = []
    for j in range(nc):
        gj = g_all[:, j * HG_DK:(j + 1) * HG_DK]
        qf = sl(q, j).astype(F32)
        kf = sl(k, j).astype(F32)
        vj = sl(v, j)
        att = jnp.zeros((c, c), F32)
        for li, m in enumerate(HG_LEVELS):
            g_l = gj[(2 + li) * c:(3 + li) * c]
            if m >= 8:
                late_blk = lambda b: ((c // m - 1 - b) if reverse else b) % 2 == 1
                qk = jnp.concatenate(
                    [(qf if late_blk(b) else kf)[b * m:(b + 1) * m] for b in range(c // m)], axis=0)
            else:
                qk = jnp.where((row & (2 * m - 1)) >= m, qf, kf)
            xl = (qk * g_l).astype(BF16)
            a_l = lax.dot_general(xl, xl, nt, preferred_element_type=F32)
            att = att + a_l * mask_ref[li]
        diag = jnp.dot((qf * kf).astype(BF16), ones, preferred_element_type=F32)
        att = att + diag * mask_ref[len(HG_LEVELS)]
        intra = jnp.dot(att.astype(BF16), vj, preferred_element_type=F32)
        g_cum = gj[0:c]
        qt = (qf * g_cum).astype(BF16)
        kh = (kf * gj[c:2 * c]).astype(BF16)
        upd = lax.dot_general(vj, kh, tn, preferred_element_type=F32)
        g_tot = g_cum[0:1] if reverse else g_cum[c - 1:c]
        parts.append((intra, qt, g_tot, upd))
    st = st_ref[...]
    outs = [None] * nc
    for j in (range(nc - 1, -1, -1) if reverse else range(nc)):
        intra, qt, g_tot, upd = parts[j]
        outs[j] = intra + lax.dot_general(qt, st.astype(BF16), nt, preferred_element_type=F32)
        st = st * g_tot + upd
    st_ref[...] = st
    return jnp.concatenate(outs, axis=0)


def _hgrn_kernel(qf_ref, kf_ref, hf_ref, lf_ref, vf_ref, qb_ref, kb_ref, hb_ref, lb_ref, vb_ref,
                 dall_ref, mask_ref, of_ref, ob_ref, st_ref, *, ts):
    gr = HG_C * HG_NC
    ngr = ts // gr

    @pl.when(pl.program_id(2) == 0)
    def _():
        st_ref[...] = jnp.zeros_like(st_ref)

    def body(i, carry):
        rows = pl.ds(pl.multiple_of(i * gr, gr), gr)
        of_ref[rows, :] = _hgrn_group(qf_ref[rows, :], kf_ref[rows, :], hf_ref[rows, :], lf_ref[rows, :],
                                      vf_ref[rows, :], st_ref.at[0], dall_ref[0], mask_ref.at[0], False)
        rows = pl.ds(pl.multiple_of((ngr - 1 - i) * gr, gr), gr)
        ob_ref[rows, :] = _hgrn_group(qb_ref[rows, :], kb_ref[rows, :], hb_ref[rows, :], lb_ref[rows, :],
                                      vb_ref[rows, :], st_ref.at[1], dall_ref[1], mask_ref.at[1], True)
        return carry

    lax.fori_loop(0, ngr, body, 0)


def _hgrn(q, lfh, lfl, kk, v, n_seq, seq, ts):
    t = q.shape[0]
    nt = seq // ts
    dall, masks = _hgrn_constants()
    dall = jnp.asarray(dall, BF16)
    masks = jnp.asarray(masks, F32)
    fwd = lambda off: pl.BlockSpec((ts, HG_DK), lambda b, h, j: (b * nt + j, h + off))
    bwd = lambda off: pl.BlockSpec((ts, HG_DK), lambda b, h, j: (b * nt + (nt - 1 - j), h + off))
    out = jax.ShapeDtypeStruct((t, HG_WIDTH), F32)
    return pl.pallas_call(
        functools.partial(_hgrn_kernel, ts=ts),
        grid=(n_seq, HG_HEADS, nt),
        in_specs=[fwd(0), fwd(0), fwd(0), fwd(0), fwd(0),
                  bwd(0), bwd(HG_HEADS), bwd(HG_HEADS), bwd(HG_HEADS), bwd(0),
                  _const_spec(dall.shape), _const_spec(masks.shape)],
        out_specs=[fwd(0), bwd(0)],
        out_shape=[out, out],
        scratch_shapes=[pltpu.VMEM((2, HG_DV, HG_DK), F32)],
        compiler_params=_cparams(("parallel", "parallel", "arbitrary")),
        name="hgrn2",
    )(q, kk, lfh, lfl, v, q, kk, lfh, lfl, v, dall, masks)


def _s5_matrices(lam_re, lam_im, log_dt, b_re, b_im, c_re, c_im, d_skip):
    hp = lax.Precision.HIGHEST
    ell, g, c, p = S5_L, S5_GROUPS, S5_GROUP, S5_STATE
    no, gpo = S5_OCTETS, S5_GPO
    dt = jnp.exp(log_dt)[..., None]

    def cpow(j):
        j = jnp.asarray(j, F32).reshape((-1, 1, 1, 1))
        mag = jnp.exp(lam_re * dt * j)
        ang = lam_im * dt * j
        return mag * jnp.cos(ang), mag * jnp.sin(ang)

    pr, pi = cpow(np.arange(ell + 1))
    abar_re, abar_im = pr[1], pi[1]
    den = lam_re * lam_re + lam_im * lam_im
    nr = abar_re - 1.0
    ni = abar_im
    cr = ((nr * lam_re + ni * lam_im) / den)[..., None]
    ci = ((ni * lam_re - nr * lam_im) / den)[..., None]
    bb_re = cr * b_re - ci * b_im
    bb_im = cr * b_im + ci * b_re

    wr = pr[..., None] * bb_re - pi[..., None] * bb_im
    wi = pr[..., None] * bb_im + pi[..., None] * bb_re
    kj = (jnp.einsum('dgcp,jdgpe->jdgce', c_re, wr, precision=hp)
          - jnp.einsum('dgcp,jdgpe->jdgce', c_im, wi, precision=hp))
    tau = np.arange(ell)
    eye = jnp.eye(gpo, dtype=F32)

    def expand(w):
        w = w.reshape(no, gpo, ell, c, 2, 2 * p).transpose(0, 2, 1, 3, 4, 5).astype(BF16)
        w = w[:, :, :, :, :, None, :] * eye.astype(BF16)[None, None, :, None, None, :, None]
        return w.reshape(no, S5_OW, S5_SW)

    skip = jnp.eye(c, dtype=F32) * d_skip.reshape(g, c, 1)
    klag = jnp.concatenate([kj[ell - 1:0:-1, 1], (kj[0, 0] + kj[0, 1] + skip)[None], kj[1:ell, 0]], axis=0)
    klag = klag.transpose(0, 1, 3, 2).reshape(2 * ell - 1, no, gpo, c, 1, c)
    bd = (klag * eye[None, None, :, None, :, None]).reshape(2 * ell - 1, no, LANES, LANES).astype(BF16)
    lag = tau[None, :] - tau[:, None] + (ell - 1)
    tz = bd[lag].transpose(2, 0, 3, 1, 4).reshape(no, S5_OW, S5_OW)

    mf_re, mf_im = wr[ell - 1 - tau, 0], wi[ell - 1 - tau, 0]
    mb_re, mb_im = wr[tau, 1], wi[tau, 1]
    mv = jnp.stack([jnp.stack([mf_re, mb_re], axis=-2), jnp.stack([mf_im, mb_im], axis=-2)], axis=-3)
    mv = expand(mv.transpose(1, 0, 5, 3, 4, 2))

    def cw(powers, d):
        p_re = pr[powers, d][:, :, None, :]
        p_im = pi[powers, d][:, :, None, :]
        return c_re[d] * p_re - c_im[d] * p_im, c_re[d] * p_im + c_im[d] * p_re

    nf_re, nf_im = cw(tau + 1, 0)
    nb_re, nb_im = cw(ell - tau, 1)
    nm = jnp.stack([jnp.stack([nf_re, nb_re], axis=-2), jnp.stack([-nf_im, -nb_im], axis=-2)], axis=-3)
    nm = jnp.swapaxes(expand(nm.transpose(1, 0, 2, 3, 4, 5)), 1, 2)

    a_re = jnp.concatenate([pr[ell, 0], pr[ell, 1]], axis=-1).reshape(no, gpo * 2 * p)
    a_im = jnp.concatenate([pi[ell, 0], pi[ell, 1]], axis=-1).reshape(no, gpo * 2 * p)
    a16 = jnp.stack([a_re, a_im], axis=1)
    return tz, mv, nm, a16.astype(F32)


def _s5_state_kernel(u_ref, mv_ref, a_ref, x_ref, v_scr, *, nseq, nchunk, segmented):
    hw = S5_SW // 2
    ntile = S5_SW // LANES
    tiles = lambda val: [val[:, k * LANES:(k + 1) * LANES] for k in range(ntile)]

    def ld(r):
        return jnp.concatenate([v_scr[k, r, :] for k in range(ntile)], axis=1)

    def st(r, val):
        for k, piece in enumerate(tiles(val)):
            v_scr[k, r, :] = piece

    for s in range(nseq):
        st(pl.ds(s, nchunk, stride=nseq),
           jnp.dot(u_ref[s * nchunk:(s + 1) * nchunk, :], mv_ref[0], preferred_element_type=F32))
    a = a_ref[0]
    ar = jnp.broadcast_to(a[0:1], (nseq, hw))
    ai = jnp.broadcast_to(a[1:2], (nseq, hw))
    lane = lax.broadcasted_iota(jnp.int32, (nseq, S5_SW), 1)
    is_fwd2 = (lane & (2 * S5_STATE - 1)) < S5_STATE
    is_fwd = is_fwd2[:, :hw]

    def rows(i):
        rf = pl.ds(pl.multiple_of(i * nseq, nseq), nseq)
        rb = pl.ds(pl.multiple_of((nchunk - 1 - i) * nseq, nseq), nseq)
        return rf, rb

    def body(i, carry):
        xr, xi = carry
        rf, rb = rows(i)
        vf = ld(rf)
        vb = ld(rb)
        x = jnp.concatenate([xr, xi], axis=1)
        st(rf, jnp.where(is_fwd2, x, vf))
        st(rb, jnp.where(is_fwd2, vb, x))
        vr = jnp.where(is_fwd, vf[:, :hw], vb[:, :hw])
        vi = jnp.where(is_fwd, vf[:, hw:], vb[:, hw:])
        return ar * xr - ai * xi + vr, ar * xi + ai * xr + vi

    zero = jnp.zeros((nseq, hw), F32)
    er, ei = lax.fori_loop(0, nchunk, body, (zero, zero))

    if segmented:
        sr, si = a[0:1], a[1:2]
        for _ in range(nchunk.bit_length() - 1):
            sr, si = sr * sr - si * si, 2.0 * sr * si
        z1 = jnp.zeros((1, hw), F32)
        cf = [(z1, z1)]
        for s in range(nseq - 1):
            pr_, pi_ = cf[-1]
            cf.append((er[s:s + 1] + sr * pr_ - si * pi_, ei[s:s + 1] + sr * pi_ + si * pr_))
        cb = [(z1, z1)]
        for s in range(nseq - 1, 0, -1):
            pr_, pi_ = cb[0]
            cb.insert(0, (er[s:s + 1] + sr * pr_ - si * pi_, ei[s:s + 1] + sr * pi_ + si * pr_))
        cr = jnp.where(is_fwd, jnp.concatenate([t[0] for t in cf], axis=0),
                       jnp.concatenate([t[0] for t in cb], axis=0))
        ci = jnp.where(is_fwd, jnp.concatenate([t[1] for t in cf], axis=0),
                       jnp.concatenate([t[1] for t in cb], axis=0))

        def fix(i, carry):
            cr, ci = carry
            rf, rb = rows(i)
            x = jnp.concatenate([cr, ci], axis=1)
            st(rf, ld(rf) + jnp.where(is_fwd2, x, 0.0))
            st(rb, ld(rb) + jnp.where(is_fwd2, 0.0, x))
            return ar * cr - ai * ci, ar * ci + ai * cr

        lax.fori_loop(0, nchunk, fix, (cr, ci))

    for s in range(nseq):
        x_ref[s * nchunk:(s + 1) * nchunk, :] = ld(pl.ds(s, nchunk, stride=nseq)).astype(BF16)


def _s5_out_kernel(u_ref, x_ref, tz_ref, nm_ref, y_ref):
    y = jnp.dot(u_ref[...], tz_ref[0], preferred_element_type=F32)
    y = y + jnp.dot(x_ref[...], nm_ref[0], preferred_element_type=F32)
    y_ref[...] = y.astype(BF16)


def _s5(u2, n_seq, seq, mats, tm_out):
    tz, mv, nm, a16 = mats
    nrow = u2.shape[0]
    nseq = S5_ROWS
    segmented = n_seq == 1
    if segmented:
        n_seq, seq = nseq, seq // nseq
    assert n_seq % nseq == 0 and seq % S5_L == 0
    nchunk = seq // S5_L
    assert nchunk & (nchunk - 1) == 0 and nchunk % 16 == 0
    m = nchunk * nseq
    blk = lambda r: pl.BlockSpec((r, S5_OW), lambda o, i: (i, o))
    mat = lambda r, c: pl.BlockSpec((1, r, c), lambda o, i: (o, 0, 0), pipeline_mode=pl.Buffered(1))
    xin = pl.pallas_call(
        functools.partial(_s5_state_kernel, nseq=nseq, nchunk=nchunk, segmented=segmented),
        grid=(S5_OCTETS, nrow // m),
        in_specs=[blk(m), mat(S5_OW, S5_SW), mat(2, S5_SW // 2)],
        out_specs=blk(m),
        out_shape=jax.ShapeDtypeStruct((nrow, S5_OCTETS * S5_SW), BF16),
        scratch_shapes=[pltpu.VMEM((S5_SW // LANES, m, LANES), F32)],
        compiler_params=_cparams(("parallel", "arbitrary")),
        name="s5_state",
    )(u2, mv, a16)
    return pl.pallas_call(
        _s5_out_kernel,
        grid=(S5_OCTETS, nrow // tm_out),
        in_specs=[blk(tm_out), blk(tm_out), mat(S5_OW, S5_OW), mat(S5_SW, S5_OW)],
        out_specs=blk(tm_out),
        out_shape=jax.ShapeDtypeStruct((nrow, S5_OCTETS * S5_OW), BF16),
        compiler_params=_cparams(("parallel", "arbitrary")),
        name="s5_out",
    )(u2, xin, tz, nm)


def _rms(x, g):
    return x * lax.rsqrt(jnp.mean(x * x, axis=-1, keepdims=True) + EPS) * g


def _out_kernel(x_ref, of_ref, ob_ref, g_ref, y2_ref, onorm_ref, gluw_ref, glub_ref,
                merge_ref, wout_ref, n2_ref, w1_ref, w2_ref, nf_ref, out_ref, y_scr):
    o = of_ref[...] + ob_ref[...]
    heads = []
    for h in range(HG_HEADS):
        oh = o[:, h * HG_DV:(h + 1) * HG_DV]
        heads.append(oh * lax.rsqrt(jnp.mean(oh * oh, axis=-1, keepdims=True) + EPS))
    g = g_ref[...].astype(F32)
    ya = jnp.concatenate(heads, axis=1) * onorm_ref[...] * (g * jax.nn.sigmoid(g))

    nrow = y_scr.shape[1] // S5_L
    for oc in range(S5_OCTETS):
        for tau in range(S5_L):
            piece = y2_ref[:, (oc * S5_L + tau) * LANES:(oc * S5_L + tau + 1) * LANES]
            y_scr[oc, pl.ds(tau, nrow, stride=S5_L), :] = piece.astype(F32)
    z = jax.nn.gelu(jnp.concatenate([y_scr[oc] for oc in range(S5_OCTETS)], axis=1))
    gate = jnp.dot(z.astype(BF16), gluw_ref[...], preferred_element_type=F32) + glub_ref[...]
    z = z * jax.nn.sigmoid(gate)
    yb = _rms(z, merge_ref[...])

    ycat = jnp.concatenate([ya, yb], axis=1).astype(BF16)
    x1 = x_ref[...] + jnp.dot(ycat, wout_ref[...], preferred_element_type=F32)
    h2 = _rms(x1, n2_ref[...]).astype(BF16)
    acc = x1
    step = D_FF // 4
    for c in range(4):
        hid = jnp.dot(h2, w1_ref[:, c * step:(c + 1) * step], preferred_element_type=F32)
        hid = jnp.square(jnp.maximum(hid, 0.0)).astype(BF16)
        acc = acc + jnp.dot(hid, w2_ref[c * step:(c + 1) * step, :], preferred_element_type=F32)
    out_ref[...] = _rms(acc, nf_ref[...])


def _outstage(x, o_f, o_b, g, y2, onorm, gluw, glub, merge, wout, n2, w1, w2, nf, tm):
    t = x.shape[0]
    row = lambda w: pl.BlockSpec((tm, w), lambda i: (i, 0))
    vec = lambda w: _const_spec((1, w))
    return pl.pallas_call(
        _out_kernel,
        grid=(t // tm,),
        in_specs=[row(D_MODEL), row(HG_WIDTH), row(HG_WIDTH), row(HG_WIDTH),
                  pl.BlockSpec((tm // S5_L, S5_OCTETS * S5_OW), lambda i: (i, 0)),
                  vec(HG_WIDTH), _const_spec((S5_WIDTH, S5_WIDTH)), vec(S5_WIDTH),
                  vec(S5_WIDTH), _const_spec((D_MODEL, D_MODEL)), vec(D_MODEL),
                  _const_spec((D_MODEL, D_FF)), _const_spec((D_FF, D_MODEL)), vec(D_MODEL)],
        out_specs=row(D_MODEL),
        out_shape=jax.ShapeDtypeStruct((t, D_MODEL), F32),
        scratch_shapes=[pltpu.VMEM((S5_OCTETS, tm, LANES), F32)],
        compiler_params=_cparams(("parallel",)),
        name="outstage",
    )(x, o_f, o_b, g, y2, onorm, gluw, glub, merge, wout, n2, w1, w2, nf)


HG_TILE = 2048
ROW_TILE_IN = 512
ROW_TILE_OUT = 512
S5_TILE_OUT = 512


def _trunk(x3, p):
    b, s, dm = x3.shape
    x = x3.reshape(b * s, dm)
    q, lfh, lfl, kk, v, g, u2 = _inproj(x, p["g1"], p["w_in"], p["lb"], ROW_TILE_IN)
    o_f, o_b = _hgrn(q, lfh, lfl, kk, v, b, s, min(HG_TILE, s))
    y2 = _s5(u2, b, s, p["s5"], min(S5_TILE_OUT, b * s // S5_L))
    out = _outstage(x, o_f, o_b, g, y2, p["onorm"], p["gluw"], p["glub"], p["merge"],
                    p["wout"], p["n2"], p["w1"], p["w2"], p["nf"], ROW_TILE_OUT)
    return out.reshape(b, s, dm)


def kernel(x_prompt, x_sample, norm1_g, w_in, hgrn_lb, hgrn_onorm_g, s5_lambda_re, s5_lambda_im,
           s5_log_dt, s5_b_re, s5_b_im, s5_c_re, s5_c_im, s5_d, s5_glu_w, s5_glu_b, s5_merge_g,
           w_out, norm2_g, w_ff1, w_ff2, norm_f_g):
    l = 0
    lbs = jnp.cumsum(jax.nn.softmax(hgrn_lb.astype(F32), axis=1), axis=1)[:, l]
    row = lambda a: a.astype(F32).reshape(1, -1)
    f = lambda a: a[l].astype(F32)
    p = {
        "g1": row(norm1_g[l]),
        "w_in": w_in[l].astype(BF16),
        "lb": lbs.reshape(1, 2 * HG_FDIM),
        "s5": _s5_matrices(f(s5_lambda_re), f(s5_lambda_im), f(s5_log_dt), f(s5_b_re), f(s5_b_im),
                           f(s5_c_re), f(s5_c_im), f(s5_d)),
        "onorm": row(hgrn_onorm_g[l]),
        "gluw": s5_glu_w[l].astype(BF16),
        "glub": row(s5_glu_b[l]),
        "merge": row(s5_merge_g[l]),
        "wout": w_out[l].astype(BF16),
        "n2": row(norm2_g[l]),
        "w1": w_ff1[l].astype(BF16),
        "w2": w_ff2[l].astype(BF16),
        "nf": row(norm_f_g),
    }
    return (_trunk(x_prompt, p), _trunk(x_sample, p))
```

```python
import functools

import numpy as np
import jax
import jax.numpy as jnp
from jax import lax
from jax.experimental import pallas as pl
from jax.experimental.pallas import tpu as pltpu

F32 = jnp.float32
BF16 = jnp.bfloat16

D_MODEL = 1024
HG_HEADS = 4
HG_DK = 128
HG_DV = 128
HG_FDIM = HG_HEADS * HG_DK
HG_WIDTH = HG_HEADS * HG_DV
S5_WIDTH = 512
S5_GROUP = 16
S5_GROUPS = S5_WIDTH // S5_GROUP
S5_STATE = 64
D_FF = 4 * D_MODEL
IN_WIDTH = 3 * HG_FDIM + 2 * HG_WIDTH + S5_WIDTH
EPS = 1e-6

LANES = 128
VMEM_LIMIT_BYTES = 56 * 1024 * 1024

HG_C = 64
HG_LEVELS = (32, 16, 8, 4, 2, 1)
HG_SUB = 8
HG_NBLK = 1 + sum(m < HG_SUB for m in HG_LEVELS)
HG_NC = 4

S5_L = 16
S5_OCTETS = S5_WIDTH // LANES
S5_GPO = LANES // S5_GROUP
S5_CW = S5_L * S5_GROUP
S5_GW = 4 * S5_STATE
S5_SW = S5_GW * S5_GPO
S5_ROWS = 8


def _cparams(sem):
    return pltpu.CompilerParams(dimension_semantics=sem, vmem_limit_bytes=VMEM_LIMIT_BYTES)


def _const_spec(shape):
    n = len(shape)
    return pl.BlockSpec(shape, lambda *_: (0,) * n, pipeline_mode=pl.Buffered(1))


def _regroup(srcs, sel, seg):
    out = None
    for k, s in enumerate(srcs):
        shift = ((k - sel) % S5_GPO) * S5_GROUP
        r = s if shift == 0 else pltpu.roll(s, shift, axis=1)
        out = r if out is None else jnp.where(seg == k, r, out)
    return out


def _inproj_kernel(x_ref, g1_ref, w_ref, lb_ref, q_ref, lfh_ref, lfl_ref, kk_ref, v_ref, g_ref, u_ref,
                   u_scr):
    x = x_ref[...]
    ms = jnp.mean(x * x, axis=-1, keepdims=True)
    h = (x * lax.rsqrt(ms + EPS) * g1_ref[...]).astype(BF16)
    proj = jnp.dot(h, w_ref[...], preferred_element_type=F32)
    o1 = HG_FDIM
    o2 = 3 * HG_FDIM
    o3 = o2 + HG_WIDTH
    o4 = o3 + HG_WIDTH
    q_ref[...] = proj[:, :o1].astype(BF16)
    z = proj[:, o1:o2]
    lb = lb_ref[...]
    f = lb + (1.0 - lb) * jax.nn.sigmoid(z)
    lf = jnp.log2(f)
    lfh = lf.astype(BF16)
    lfh_ref[...] = lfh
    lfl_ref[...] = (lf - lfh.astype(F32)).astype(BF16)
    kk_ref[...] = ((1.0 - lb) * jax.nn.sigmoid(-z)).astype(BF16)
    v_ref[...] = proj[:, o2:o3].astype(BF16)
    g_ref[...] = proj[:, o3:o4].astype(BF16)
    nrow = u_scr.shape[1] // S5_L
    seg = lax.broadcasted_iota(jnp.int32, (nrow, LANES), 1) // S5_GROUP
    for o in range(S5_OCTETS):
        u_scr[o] = proj[:, o4 + o * LANES:o4 + (o + 1) * LANES]
        src = [u_scr[o, pl.ds(tau, nrow, stride=S5_L), :] for tau in range(S5_L)]
        for half in range(S5_L // S5_GPO):
            for i in range(S5_GPO):
                u_ref[o * S5_GPO + i, :, half * LANES:(half + 1) * LANES] = _regroup(
                    src[half * S5_GPO:(half + 1) * S5_GPO], i, seg).astype(BF16)


def _inproj(x, g1, w_in, lb, tm):
    t = x.shape[0]
    row = lambda w: pl.BlockSpec((tm, w), lambda i: (i, 0))
    outs = [
        jax.ShapeDtypeStruct((t, HG_FDIM), BF16),
        jax.ShapeDtypeStruct((t, 2 * HG_FDIM), BF16),
        jax.ShapeDtypeStruct((t, 2 * HG_FDIM), BF16),
        jax.ShapeDtypeStruct((t, 2 * HG_FDIM), BF16),
        jax.ShapeDtypeStruct((t, HG_WIDTH), BF16),
        jax.ShapeDtypeStruct((t, HG_WIDTH), BF16),
        jax.ShapeDtypeStruct((S5_GROUPS, t // S5_L, S5_CW), BF16),
    ]
    return pl.pallas_call(
        _inproj_kernel,
        grid=(t // tm,),
        in_specs=[row(D_MODEL), _const_spec((1, D_MODEL)), _const_spec((D_MODEL, IN_WIDTH)),
                  _const_spec((1, 2 * HG_FDIM))],
        out_specs=[row(HG_FDIM), row(2 * HG_FDIM), row(2 * HG_FDIM), row(2 * HG_FDIM),
                   row(HG_WIDTH), row(HG_WIDTH),
                   pl.BlockSpec((S5_GROUPS, tm // S5_L, S5_CW), lambda i: (0, i, 0))],
        out_shape=outs,
        scratch_shapes=[pltpu.VMEM((S5_OCTETS, tm, LANES), F32)],
        compiler_params=_cparams(("parallel",)),
        name="inproj",
    )(x, g1, w_in, lb)


def _hgrn_constants():
    c = HG_C
    idx = np.arange(c)
    blocks = [(idx[None, :] <= idx[:, None])]
    masks = []
    for m in HG_LEVELS:
        rho = (idx // (2 * m)) * (2 * m) + m
        late = idx >= rho
        x = idx[None, :]
        r = idx[:, None]
        d = np.where(late[:, None], (x >= rho[:, None]) & (x <= r), (x > r) & (x < rho[:, None]))
        if m < HG_SUB:
            blocks.append(d)
        same = (idx[:, None] // (2 * m)) == (idx[None, :] // (2 * m))
        masks.append(same & late[:, None] & (~late)[None, :])
    d_fwd = np.concatenate(blocks, axis=0).astype(np.float32)
    d_bwd = np.concatenate([b[::-1, ::-1] for b in blocks], axis=0).astype(np.float32)
    m_fwd = np.stack(masks).astype(np.float32)
    m_bwd = m_fwd[:, ::-1, ::-1]
    dall = np.stack([d_fwd, d_bwd])
    dall = np.concatenate([dall, dall], axis=-1)
    return dall, np.stack([m_fwd, m_bwd])


def _hgrn_group(q, k, lfh, lfl, v, st_ref, dall, mask_ref, reverse):
    c, nc = HG_C, HG_NC
    nt = (((1,), (1,)), ((), ()))
    tn = (((0,), (0,)), ((), ()))
    sl = lambda a, j: a[j * c:(j + 1) * c]
    lf2 = jnp.concatenate(
        [jnp.concatenate([sl(lfh, j), sl(lfl, j)], axis=0) for j in range(nc)], axis=1)
    e_all = jnp.dot(dall, lf2, preferred_element_type=F32)
    row = lax.broadcasted_iota(jnp.int32, (c, HG_DK), 0)
    if reverse:
        row = (c - 1) - row
    sub = HG_SUB
    ones = jnp.ones((HG_DK, c), BF16)
    eye = lax.broadcasted_iota(jnp.int32, (c, c), 0) == lax.broadcasted_iota(jnp.int32, (c, c), 1)
    parts = []
    for j in range(nc):
        ej = e_all[:, j * HG_DK:(j + 1) * HG_DK]
        bc = ej[0:c]
        qf = sl(q, j).astype(F32)
        kf = sl(k, j).astype(F32)
        vj = sl(v, j)
        att = [jnp.zeros((sub, c), F32) for _ in range(c // sub)]
        for li, m in enumerate(HG_LEVELS):
            msk = mask_ref[li]
            if m >= sub:
                pieces = []
                for pb in range(c // (2 * m)):
                    lo, mid, hi = pb * 2 * m, pb * 2 * m + m, (pb + 1) * 2 * m
                    ref = bc[mid:mid + 1] if reverse else bc[mid - 1:mid]
                    first, second = bc[lo:mid], bc[mid:hi]
                    pieces += ([first - ref, ref - second] if reverse else [ref - first, second - ref])
                g_l = jnp.exp2(jnp.concatenate(pieces, axis=0))
                late = [b for b in range(c // m) if ((c // m - 1 - b) if reverse else b) % 2 == 1]
                qk = jnp.concatenate(
                    [(qf if b in late else kf)[b * m:(b + 1) * m] for b in range(c // m)], axis=0)
                xl = (qk * g_l).astype(BF16)
                xq = jnp.concatenate([xl[b * m:(b + 1) * m] for b in late], axis=0)
                a_l = lax.dot_general(xq, xl, nt, preferred_element_type=F32)
                for n, b in enumerate(late):
                    for r in range(m // sub):
                        t = b * (m // sub) + r
                        lo = n * m + r * sub
                        att[t] = att[t] + a_l[lo:lo + sub] * msk[t * sub:(t + 1) * sub]
            else:
                nf = 1 + li - sum(mm >= sub for mm in HG_LEVELS)
                g_l = jnp.exp2(ej[nf * c:(nf + 1) * c])
                xl = (jnp.where((row & (2 * m - 1)) >= m, qf, kf) * g_l).astype(BF16)
                a_l = lax.dot_general(xl, xl, nt, preferred_element_type=F32)
                for t in range(c // sub):
                    att[t] = att[t] + a_l[t * sub:(t + 1) * sub] * msk[t * sub:(t + 1) * sub]
        dg = jnp.dot((qf * kf).astype(BF16), ones, preferred_element_type=F32)
        att = jnp.concatenate(att, axis=0) + jnp.where(eye, dg, 0.0)
        intra = jnp.dot(att.astype(BF16), vj, preferred_element_type=F32)
        g_cum = jnp.exp2(bc)
        qt = (qf * g_cum).astype(BF16)
        tot = bc[0:1] if reverse else bc[c - 1:c]
        kh = (kf * jnp.exp2(tot - bc)).astype(BF16)
        upd = lax.dot_general(vj, kh, tn, preferred_element_type=F32)
        g_tot = g_cum[0:1] if reverse else g_cum[c - 1:c]
        parts.append((intra, qt, g_tot, upd))
    st = st_ref[...]
    outs = [None] * nc
    for j in (range(nc - 1, -1, -1) if reverse else range(nc)):
        intra, qt, g_tot, upd = parts[j]
        outs[j] = intra + lax.dot_general(qt, st.astype(BF16), nt, preferred_element_type=F32)
        st = st * g_tot + upd
    st_ref[...] = st
    return jnp.concatenate(outs, axis=0)


def _hgrn_kernel(qf_ref, kf_ref, hf_ref, lf_ref, vf_ref, qb_ref, kb_ref, hb_ref, lb_ref, vb_ref,
                 dall_ref, mask_ref, of_ref, ob_ref, st_ref, *, ts):
    gr = HG_C * HG_NC
    ngr = ts // gr

    @pl.when(pl.program_id(2) == 0)
    def _():
        st_ref[...] = jnp.zeros_like(st_ref)

    def body(i, carry):
        rows = pl.ds(pl.multiple_of(i * gr, gr), gr)
        of_ref[rows, :] = _hgrn_group(qf_ref[rows, :], kf_ref[rows, :], hf_ref[rows, :], lf_ref[rows, :],
                                      vf_ref[rows, :], st_ref.at[0], dall_ref[0], mask_ref.at[0], False)
        rows = pl.ds(pl.multiple_of((ngr - 1 - i) * gr, gr), gr)
        ob_ref[rows, :] = _hgrn_group(qb_ref[rows, :], kb_ref[rows, :], hb_ref[rows, :], lb_ref[rows, :],
                                      vb_ref[rows, :], st_ref.at[1], dall_ref[1], mask_ref.at[1], True)
        return carry

    lax.fori_loop(0, ngr, body, 0)


def _hgrn(q, lfh, lfl, kk, v, n_seq, seq, ts):
    t = q.shape[0]
    nt = seq // ts
    dall, masks = _hgrn_constants()
    dall = jnp.asarray(dall, BF16)
    masks = jnp.asarray(masks, F32)
    fwd = lambda off: pl.BlockSpec((ts, HG_DK), lambda b, h, j: (b * nt + j, h + off))
    bwd = lambda off: pl.BlockSpec((ts, HG_DK), lambda b, h, j: (b * nt + (nt - 1 - j), h + off))
    out = jax.ShapeDtypeStruct((t, HG_WIDTH), F32)
    return pl.pallas_call(
        functools.partial(_hgrn_kernel, ts=ts),
        grid=(n_seq, HG_HEADS, nt),
        in_specs=[fwd(0), fwd(0), fwd(0), fwd(0), fwd(0),
                  bwd(0), bwd(HG_HEADS), bwd(HG_HEADS), bwd(HG_HEADS), bwd(0),
                  _const_spec(dall.shape), _const_spec(masks.shape)],
        out_specs=[fwd(0), bwd(0)],
        out_shape=[out, out],
        scratch_shapes=[pltpu.VMEM((2, HG_DV, HG_DK), F32)],
        compiler_params=_cparams(("parallel", "parallel", "arbitrary")),
        name="hgrn2",
    )(q, kk, lfh, lfl, v, q, kk, lfh, lfl, v, dall, masks)


def _s5_matrices(lam_re, lam_im, log_dt, b_re, b_im, c_re, c_im, d_skip):
    hp = lax.Precision.HIGHEST
    ell, g, c, p = S5_L, S5_GROUPS, S5_GROUP, S5_STATE
    no, gpo = S5_OCTETS, S5_GPO
    dt = jnp.exp(log_dt)[..., None]

    def cpow(j):
        j = jnp.asarray(j, F32).reshape((-1, 1, 1, 1))
        mag = jnp.exp(lam_re * dt * j)
        ang = lam_im * dt * j
        return mag * jnp.cos(ang), mag * jnp.sin(ang)

    pr, pi = cpow(np.arange(ell + 1))
    abar_re, abar_im = pr[1], pi[1]
    den = lam_re * lam_re + lam_im * lam_im
    nr = abar_re - 1.0
    ni = abar_im
    cr = ((nr * lam_re + ni * lam_im) / den)[..., None]
    ci = ((ni * lam_re - nr * lam_im) / den)[..., None]
    bb_re = cr * b_re - ci * b_im
    bb_im = cr * b_im + ci * b_re

    wr = pr[..., None] * bb_re - pi[..., None] * bb_im
    wi = pr[..., None] * bb_im + pi[..., None] * bb_re
    kj = (jnp.einsum('dgcp,jdgpe->jdgce', c_re, wr, precision=hp)
          - jnp.einsum('dgcp,jdgpe->jdgce', c_im, wi, precision=hp))
    tau = np.arange(ell)

    skip = jnp.eye(c, dtype=F32) * d_skip.reshape(g, c, 1)
    klag = jnp.concatenate([kj[ell - 1:0:-1, 1], (kj[0, 0] + kj[0, 1] + skip)[None], kj[1:ell, 0]], axis=0)
    lag = tau[None, :] - tau[:, None] + (ell - 1)
    tz = klag[lag].transpose(2, 0, 4, 1, 3).reshape(g, S5_CW, S5_CW)

    mf_re, mf_im = wr[ell - 1 - tau, 0], wi[ell - 1 - tau, 0]
    mb_re, mb_im = wr[tau, 1], wi[tau, 1]
    mv = jnp.stack([jnp.stack([mf_re, mb_re], axis=-2), jnp.stack([mf_im, mb_im], axis=-2)], axis=-3)
    mv = mv.transpose(1, 0, 5, 3, 4, 2).reshape(g, S5_CW, S5_GW)

    def cw(powers, d):
        p_re = pr[powers, d][:, :, None, :]
        p_im = pi[powers, d][:, :, None, :]
        return c_re[d] * p_re - c_im[d] * p_im, c_re[d] * p_im + c_im[d] * p_re

    nf_re, nf_im = cw(tau + 1, 0)
    nb_re, nb_im = cw(ell - tau, 1)
    nm = jnp.stack([jnp.stack([nf_re, nb_re], axis=-2), jnp.stack([-nf_im, -nb_im], axis=-2)], axis=-3)
    nm = nm.transpose(1, 3, 4, 5, 0, 2).reshape(g, S5_GW, S5_CW)

    a_re = jnp.concatenate([pr[ell, 0], pr[ell, 1]], axis=-1).reshape(no, gpo * 2 * p)
    a_im = jnp.concatenate([pi[ell, 0], pi[ell, 1]], axis=-1).reshape(no, gpo * 2 * p)
    a16 = jnp.stack([a_re, a_im], axis=1)
    return tz.astype(BF16), mv.astype(BF16), nm.astype(BF16), a16.astype(F32)


def _s5_state_kernel(u_ref, mv_ref, a_ref, x_ref, v_scr, *, nseq, nchunk, segmented):
    hw = S5_SW // 2
    ntile = S5_SW // LANES
    gpo = S5_GPO

    def ld(r):
        return jnp.concatenate([v_scr[k, r, :] for k in range(ntile)], axis=1)

    def st(r, val):
        for k in range(ntile):
            v_scr[k, r, :] = val[:, k * LANES:(k + 1) * LANES]

    for i in range(gpo):
        inc = jnp.dot(u_ref[i], mv_ref[i], preferred_element_type=F32)
        for s in range(nseq):
            dst = pl.ds(s, nchunk, stride=nseq)
            v_scr[i, dst, :] = inc[s * nchunk:(s + 1) * nchunk, :LANES]
            v_scr[gpo + i, dst, :] = inc[s * nchunk:(s + 1) * nchunk, LANES:]
    a = a_ref[0]
    ar = jnp.broadcast_to(a[0:1], (nseq, hw))
    ai = jnp.broadcast_to(a[1:2], (nseq, hw))
    lane = lax.broadcasted_iota(jnp.int32, (nseq, S5_SW), 1)
    is_fwd2 = (lane & (2 * S5_STATE - 1)) < S5_STATE
    is_fwd = is_fwd2[:, :hw]

    def rows(i):
        rf = pl.ds(pl.multiple_of(i * nseq, nseq), nseq)
        rb = pl.ds(pl.multiple_of((nchunk - 1 - i) * nseq, nseq), nseq)
        return rf, rb

    def body(i, carry):
        xr, xi = carry
        rf, rb = rows(i)
        vf = ld(rf)
        vb = ld(rb)
        x = jnp.concatenate([xr, xi], axis=1)
        st(rf, jnp.where(is_fwd2, x, vf))
        st(rb, jnp.where(is_fwd2, vb, x))
        vr = jnp.where(is_fwd, vf[:, :hw], vb[:, :hw])
        vi = jnp.where(is_fwd, vf[:, hw:], vb[:, hw:])
        return ar * xr - ai * xi + vr, ar * xi + ai * xr + vi

    zero = jnp.zeros((nseq, hw), F32)
    er, ei = lax.fori_loop(0, nchunk, body, (zero, zero))

    if segmented:
        sr, si = a[0:1], a[1:2]
        for _ in range(nchunk.bit_length() - 1):
            sr, si = sr * sr - si * si, 2.0 * sr * si
        z1 = jnp.zeros((1, hw), F32)
        cf = [(z1, z1)]
        for s in range(nseq - 1):
            pr_, pi_ = cf[-1]
            cf.append((er[s:s + 1] + sr * pr_ - si * pi_, ei[s:s + 1] + sr * pi_ + si * pr_))
        cb = [(z1, z1)]
        for s in range(nseq - 1, 0, -1):
            pr_, pi_ = cb[0]
            cb.insert(0, (er[s:s + 1] + sr * pr_ - si * pi_, ei[s:s + 1] + sr * pi_ + si * pr_))
        cr = jnp.where(is_fwd, jnp.concatenate([t[0] for t in cf], axis=0),
                       jnp.concatenate([t[0] for t in cb], axis=0))
        ci = jnp.where(is_fwd, jnp.concatenate([t[1] for t in cf], axis=0),
                       jnp.concatenate([t[1] for t in cb], axis=0))

        def fix(i, carry):
            cr, ci = carry
            rf, rb = rows(i)
            x = jnp.concatenate([cr, ci], axis=1)
            st(rf, ld(rf) + jnp.where(is_fwd2, x, 0.0))
            st(rb, ld(rb) + jnp.where(is_fwd2, 0.0, x))
            return ar * cr - ai * ci, ar * ci + ai * cr

        lax.fori_loop(0, nchunk, fix, (cr, ci))

    for i in range(gpo):
        for s in range(nseq):
            src = pl.ds(s, nchunk, stride=nseq)
            x_ref[i, s * nchunk:(s + 1) * nchunk, :] = jnp.concatenate(
                [v_scr[i, src, :], v_scr[gpo + i, src, :]], axis=1).astype(BF16)


def _s5_out_kernel(u_ref, x_ref, tz_ref, nm_ref, y_ref):
    for i in range(S5_GPO):
        y = jnp.dot(u_ref[i], tz_ref[i], preferred_element_type=F32)
        y = y + jnp.dot(x_ref[i], nm_ref[i], preferred_element_type=F32)
        y_ref[i] = y.astype(BF16)


def _s5(ug, n_seq, seq, mats, tm_out):
    tz, mv, nm, a16 = mats
    nrow = ug.shape[1]
    nseq = S5_ROWS
    segmented = n_seq == 1
    if segmented:
        n_seq, seq = nseq, seq // nseq
    assert n_seq % nseq == 0 and seq % S5_L == 0
    nchunk = seq // S5_L
    assert nchunk & (nchunk - 1) == 0 and nchunk % 16 == 0
    m = nchunk * nseq
    blk = lambda r: pl.BlockSpec((S5_GPO, r, S5_CW), lambda o, i: (o, i, 0))
    mat = lambda r, c: pl.BlockSpec((S5_GPO, r, c), lambda o, i: (o, 0, 0))
    xin = pl.pallas_call(
        functools.partial(_s5_state_kernel, nseq=nseq, nchunk=nchunk, segmented=segmented),
        grid=(S5_OCTETS, nrow // m),
        in_specs=[blk(m), mat(S5_CW, S5_GW),
                  pl.BlockSpec((1, 2, S5_SW // 2), lambda o, i: (o, 0, 0))],
        out_specs=blk(m),
        out_shape=jax.ShapeDtypeStruct((S5_GROUPS, nrow, S5_GW), BF16),
        scratch_shapes=[pltpu.VMEM((S5_SW // LANES, m, LANES), F32)],
        compiler_params=_cparams(("parallel", "arbitrary")),
        name="s5_state",
    )(ug, mv, a16)
    return pl.pallas_call(
        _s5_out_kernel,
        grid=(S5_OCTETS, nrow // tm_out),
        in_specs=[blk(tm_out), blk(tm_out), mat(S5_CW, S5_CW), mat(S5_GW, S5_CW)],
        out_specs=blk(tm_out),
        out_shape=jax.ShapeDtypeStruct((S5_GROUPS, nrow, S5_CW), BF16),
        compiler_params=_cparams(("parallel", "arbitrary")),
        name="s5_out",
    )(ug, xin, tz, nm)


def _rms(x, g):
    return x * lax.rsqrt(jnp.mean(x * x, axis=-1, keepdims=True) + EPS) * g


def _out_kernel(x_ref, of_ref, ob_ref, g_ref, y2_ref, onorm_ref, gluw_ref, glub_ref,
                merge_ref, wout_ref, n2_ref, w1_ref, w2_ref, nf_ref, out_ref, y_scr):
    o = of_ref[...] + ob_ref[...]
    heads = []
    for h in range(HG_HEADS):
        oh = o[:, h * HG_DV:(h + 1) * HG_DV]
        heads.append(oh * lax.rsqrt(jnp.mean(oh * oh, axis=-1, keepdims=True) + EPS))
    g = g_ref[...].astype(F32)
    ya = jnp.concatenate(heads, axis=1) * onorm_ref[...] * (g * jax.nn.sigmoid(g))

    nrow = y_scr.shape[1] // S5_L
    seg = lax.broadcasted_iota(jnp.int32, (nrow, LANES), 1) // S5_GROUP
    for oc in range(S5_OCTETS):
        for half in range(S5_L // S5_GPO):
            src = [y2_ref[oc * S5_GPO + i, :, half * LANES:(half + 1) * LANES].astype(F32)
                   for i in range(S5_GPO)]
            for t8 in range(S5_GPO):
                y_scr[oc, pl.ds(half * S5_GPO + t8, nrow, stride=S5_L), :] = _regroup(src, t8, seg)
    z = jax.nn.gelu(jnp.concatenate([y_scr[oc] for oc in range(S5_OCTETS)], axis=1))
    gate = jnp.dot(z.astype(BF16), gluw_ref[...], preferred_element_type=F32) + glub_ref[...]
    z = z * jax.nn.sigmoid(gate)
    yb = _rms(z, merge_ref[...])

    ycat = jnp.concatenate([ya, yb], axis=1).astype(BF16)
    x1 = x_ref[...] + jnp.dot(ycat, wout_ref[...], preferred_element_type=F32)
    h2 = _rms(x1, n2_ref[...]).astype(BF16)
    acc = x1
    step = D_FF // 4
    for c in range(4):
        hid = jnp.dot(h2, w1_ref[:, c * step:(c + 1) * step], preferred_element_type=F32)
        hid = jnp.square(jnp.maximum(hid, 0.0)).astype(BF16)
        acc = acc + jnp.dot(hid, w2_ref[c * step:(c + 1) * step, :], preferred_element_type=F32)
    out_ref[...] = _rms(acc, nf_ref[...])


def _outstage(x, o_f, o_b, g, y2, onorm, gluw, glub, merge, wout, n2, w1, w2, nf, tm):
    t = x.shape[0]
    row = lambda w: pl.BlockSpec((tm, w), lambda i: (i, 0))
    vec = lambda w: _const_spec((1, w))
    return pl.pallas_call(
        _out_kernel,
        grid=(t // tm,),
        in_specs=[row(D_MODEL), row(HG_WIDTH), row(HG_WIDTH), row(HG_WIDTH),
                  pl.BlockSpec((S5_GROUPS, tm // S5_L, S5_CW), lambda i: (0, i, 0)),
                  vec(HG_WIDTH), _const_spec((S5_WIDTH, S5_WIDTH)), vec(S5_WIDTH),
                  vec(S5_WIDTH), _const_spec((D_MODEL, D_MODEL)), vec(D_MODEL),
                  _const_spec((D_MODEL, D_FF)), _const_spec((D_FF, D_MODEL)), vec(D_MODEL)],
        out_specs=row(D_MODEL),
        out_shape=jax.ShapeDtypeStruct((t, D_MODEL), F32),
        scratch_shapes=[pltpu.VMEM((S5_OCTETS, tm, LANES), F32)],
        compiler_params=_cparams(("parallel",)),
        name="outstage",
    )(x, o_f, o_b, g, y2, onorm, gluw, glub, merge, wout, n2, w1, w2, nf)


HG_TILE = 2048
ROW_TILE_IN = 512
ROW_TILE_OUT = 512
S5_TILE_OUT = 512


def _trunk(x3, p):
    b, s, dm = x3.shape
    x = x3.reshape(b * s, dm)
    q, lfh, lfl, kk, v, g, u2 = _inproj(x, p["g1"], p["w_in"], p["lb"], ROW_TILE_IN)
    o_f, o_b = _hgrn(q, lfh, lfl, kk, v, b, s, min(HG_TILE, s))
    y2 = _s5(u2, b, s, p["s5"], min(S5_TILE_OUT, b * s // S5_L))
    out = _outstage(x, o_f, o_b, g, y2, p["onorm"], p["gluw"], p["glub"], p["merge"],
                    p["wout"], p["n2"], p["w1"], p["w2"], p["nf"], ROW_TILE_OUT)
    return out.reshape(b, s, dm)


def kernel(x_prompt, x_sample, norm1_g, w_in, hgrn_lb, hgrn_onorm_g, s5_lambda_re, s5_lambda_im,
           s5_log_dt, s5_b_re, s5_b_im, s5_c_re, s5_c_im, s5_d, s5_glu_w, s5_glu_b, s5_merge_g,
           w_out, norm2_g, w_ff1, w_ff2, norm_f_g):
    l = 0
    lbs = jnp.cumsum(jax.nn.softmax(hgrn_lb.astype(F32), axis=1), axis=1)[:, l]
    row = lambda a: a.astype(F32).reshape(1, -1)
    f = lambda a: a[l].astype(F32)
    p = {
        "g1": row(norm1_g[l]),
        "w_in": w_in[l].astype(BF16),
        "lb": lbs.reshape(1, 2 * HG_FDIM),
        "s5": _s5_matrices(f(s5_lambda_re), f(s5_lambda_im), f(s5_log_dt), f(s5_b_re), f(s5_b_im),
                           f(s5_c_re), f(s5_c_im), f(s5_d)),
        "onorm": row(hgrn_onorm_g[l]),
        "gluw": s5_glu_w[l].astype(BF16),
        "glub": row(s5_glu_b[l]),
        "merge": row(s5_merge_g[l]),
        "wout": w_out[l].astype(BF16),
        "n2": row(norm2_g[l]),
        "w1": w_ff1[l].astype(BF16),
        "w2": w_ff2[l].astype(BF16),
        "nf": row(norm_f_g),
    }
    return (_trunk(x_prompt, p), _trunk(x_sample, p))
```

```python
import functools

import numpy as np
import jax
import jax.numpy as jnp
from jax import lax
from jax.experimental import pallas as pl
from jax.experimental.pallas import tpu as pltpu

F32 = jnp.float32
BF16 = jnp.bfloat16

D_MODEL = 1024
HG_HEADS = 4
HG_DK = 128
HG_DV = 128
HG_FDIM = HG_HEADS * HG_DK
HG_WIDTH = HG_HEADS * HG_DV
S5_WIDTH = 512
S5_GROUP = 16
S5_GROUPS = S5_WIDTH // S5_GROUP
S5_STATE = 64
D_FF = 4 * D_MODEL
IN_WIDTH = 3 * HG_FDIM + 2 * HG_WIDTH + S5_WIDTH
EPS = 1e-6

LANES = 128
VMEM_LIMIT_BYTES = 56 * 1024 * 1024

HG_C = 64
HG_LEVELS = (32, 16, 8, 4, 2, 1)
HG_SUB = 8
HG_NBLK = 1 + sum(m < HG_SUB for m in HG_LEVELS)
HG_NC = 8

S5_L = 16
S5_OCTETS = S5_WIDTH // LANES
S5_GPO = LANES // S5_GROUP
S5_CW = S5_L * S5_GROUP
S5_GW = 4 * S5_STATE
S5_SW = S5_GW * S5_GPO
S5_ROWS = 8


def _cparams(sem):
    return pltpu.CompilerParams(dimension_semantics=sem, vmem_limit_bytes=VMEM_LIMIT_BYTES)


def _const_spec(shape):
    n = len(shape)
    return pl.BlockSpec(shape, lambda *_: (0,) * n, pipeline_mode=pl.Buffered(1))


def _regroup(srcs, sel, seg):
    out = None
    for k, s in enumerate(srcs):
        shift = ((k - sel) % S5_GPO) * S5_GROUP
        r = s if shift == 0 else pltpu.roll(s, shift, axis=1)
        out = r if out is None else jnp.where(seg == k, r, out)
    return out


def _inproj_kernel(x_ref, g1_ref, w_ref, lb_ref, q_ref, lfh_ref, lfl_ref, kk_ref, v_ref, g_ref, u_ref,
                   u_scr):
    x = x_ref[...]
    ms = jnp.mean(x * x, axis=-1, keepdims=True)
    h = (x * lax.rsqrt(ms + EPS) * g1_ref[...]).astype(BF16)
    o1 = HG_FDIM
    o2 = 3 * HG_FDIM
    o3 = o2 + HG_WIDTH
    o4 = o3 + HG_WIDTH
    proj = lambda lo, hi: jnp.dot(h, w_ref[:, lo:hi], preferred_element_type=F32)
    nrow = u_scr.shape[1] // S5_L
    seg = lax.broadcasted_iota(jnp.int32, (nrow, LANES), 1) // S5_GROUP
    pu = proj(o4, IN_WIDTH)
    for o in range(S5_OCTETS):
        u_scr[o] = pu[:, o * LANES:(o + 1) * LANES]
        src = [u_scr[o, pl.ds(tau, nrow, stride=S5_L), :] for tau in range(S5_L)]
        for half in range(S5_L // S5_GPO):
            for i in range(S5_GPO):
                u_ref[o * S5_GPO + i, :, half * LANES:(half + 1) * LANES] = _regroup(
                    src[half * S5_GPO:(half + 1) * S5_GPO], i, seg).astype(BF16)
    for d in range(2):
        z = proj(o1 + d * HG_FDIM, o1 + (d + 1) * HG_FDIM)
        cols = slice(d * HG_FDIM, (d + 1) * HG_FDIM)
        lb = lb_ref[:, cols]
        f = lb + (1.0 - lb) * jax.nn.sigmoid(z)
        lf = jnp.log2(f)
        lfh = lf.astype(BF16)
        lfh_ref[:, cols] = lfh
        lfl_ref[:, cols] = (lf - lfh.astype(F32)).astype(BF16)
        kk_ref[:, cols] = ((1.0 - lb) * jax.nn.sigmoid(-z)).astype(BF16)
    q_ref[...] = proj(0, o1).astype(BF16)
    v_ref[...] = proj(o2, o3).astype(BF16)
    g_ref[...] = proj(o3, o4).astype(BF16)


def _inproj(x, g1, w_in, lb, tm):
    t = x.shape[0]
    row = lambda w: pl.BlockSpec((tm, w), lambda i: (i, 0))
    outs = [
        jax.ShapeDtypeStruct((t, HG_FDIM), BF16),
        jax.ShapeDtypeStruct((t, 2 * HG_FDIM), BF16),
        jax.ShapeDtypeStruct((t, 2 * HG_FDIM), BF16),
        jax.ShapeDtypeStruct((t, 2 * HG_FDIM), BF16),
        jax.ShapeDtypeStruct((t, HG_WIDTH), BF16),
        jax.ShapeDtypeStruct((t, HG_WIDTH), BF16),
        jax.ShapeDtypeStruct((S5_GROUPS, t // S5_L, S5_CW), BF16),
    ]
    return pl.pallas_call(
        _inproj_kernel,
        grid=(t // tm,),
        in_specs=[row(D_MODEL), _const_spec((1, D_MODEL)), _const_spec((D_MODEL, IN_WIDTH)),
                  _const_spec((1, 2 * HG_FDIM))],
        out_specs=[row(HG_FDIM), row(2 * HG_FDIM), row(2 * HG_FDIM), row(2 * HG_FDIM),
                   row(HG_WIDTH), row(HG_WIDTH),
                   pl.BlockSpec((S5_GROUPS, tm // S5_L, S5_CW), lambda i: (0, i, 0))],
        out_shape=outs,
        scratch_shapes=[pltpu.VMEM((S5_OCTETS, tm, LANES), F32)],
        compiler_params=_cparams(("parallel",)),
        name="inproj",
    )(x, g1, w_in, lb)


def _hgrn_constants():
    c = HG_C
    idx = np.arange(c)
    blocks = [(idx[None, :] <= idx[:, None])]
    masks = []
    for m in HG_LEVELS:
        rho = (idx // (2 * m)) * (2 * m) + m
        late = idx >= rho
        x = idx[None, :]
        r = idx[:, None]
        d = np.where(late[:, None], (x >= rho[:, None]) & (x <= r), (x > r) & (x < rho[:, None]))
        if m < HG_SUB:
            blocks.append(d)
        same = (idx[:, None] // (2 * m)) == (idx[None, :] // (2 * m))
        masks.append(same & late[:, None] & (~late)[None, :])
    d_fwd = np.concatenate(blocks, axis=0).astype(np.float32)
    d_bwd = np.concatenate([b[::-1, ::-1] for b in blocks], axis=0).astype(np.float32)
    m_fwd = np.stack(masks).astype(np.float32)
    m_bwd = m_fwd[:, ::-1, ::-1]
    dall = np.stack([d_fwd, d_bwd])
    dall = np.concatenate([dall, dall], axis=-1)
    return dall, np.stack([m_fwd, m_bwd])


def _hgrn_group(q, k, lfh, lfl, v, st_ref, dall, mask_ref, reverse):
    c, nc = HG_C, HG_NC
    nt = (((1,), (1,)), ((), ()))
    tn = (((0,), (0,)), ((), ()))
    sl = lambda a, j: a[j * c:(j + 1) * c]
    lf2 = jnp.concatenate(
        [jnp.concatenate([sl(lfh, j), sl(lfl, j)], axis=0) for j in range(nc)], axis=1)
    e_all = jnp.dot(dall, lf2, preferred_element_type=F32)
    row = lax.broadcasted_iota(jnp.int32, (c, HG_DK), 0)
    if reverse:
        row = (c - 1) - row
    sub = HG_SUB
    ones = jnp.ones((HG_DK, c), BF16)
    eye = lax.broadcasted_iota(jnp.int32, (c, c), 0) == lax.broadcasted_iota(jnp.int32, (c, c), 1)
    parts = []
    for j in range(nc):
        ej = e_all[:, j * HG_DK:(j + 1) * HG_DK]
        bc = ej[0:c]
        qf = sl(q, j).astype(F32)
        kf = sl(k, j).astype(F32)
        vj = sl(v, j)
        att = [jnp.zeros((sub, c), F32) for _ in range(c // sub)]
        for li, m in enumerate(HG_LEVELS):
            msk = mask_ref[li]
            if m >= sub:
                pieces = []
                for pb in range(c // (2 * m)):
                    lo, mid, hi = pb * 2 * m, pb * 2 * m + m, (pb + 1) * 2 * m
                    ref = bc[mid:mid + 1] if reverse else bc[mid - 1:mid]
                    first, second = bc[lo:mid], bc[mid:hi]
                    pieces += ([first - ref, ref - second] if reverse else [ref - first, second - ref])
                g_l = jnp.exp2(jnp.concatenate(pieces, axis=0))
                late = [b for b in range(c // m) if ((c // m - 1 - b) if reverse else b) % 2 == 1]
                qk = jnp.concatenate(
                    [(qf if b in late else kf)[b * m:(b + 1) * m] for b in range(c // m)], axis=0)
                xl = (qk * g_l).astype(BF16)
                xq = jnp.concatenate([xl[b * m:(b + 1) * m] for b in late], axis=0)
                a_l = lax.dot_general(xq, xl, nt, preferred_element_type=F32)
                for n, b in enumerate(late):
                    for r in range(m // sub):
                        t = b * (m // sub) + r
                        lo = n * m + r * sub
                        att[t] = att[t] + a_l[lo:lo + sub] * msk[t * sub:(t + 1) * sub]
            else:
                nf = 1 + li - sum(mm >= sub for mm in HG_LEVELS)
                g_l = jnp.exp2(ej[nf * c:(nf + 1) * c])
                xl = (jnp.where((row & (2 * m - 1)) >= m, qf, kf) * g_l).astype(BF16)
                a_l = lax.dot_general(xl, xl, nt, preferred_element_type=F32)
                for t in range(c // sub):
                    att[t] = att[t] + a_l[t * sub:(t + 1) * sub] * msk[t * sub:(t + 1) * sub]
        dg = jnp.dot((qf * kf).astype(BF16), ones, preferred_element_type=F32)
        att = jnp.concatenate(att, axis=0) + jnp.where(eye, dg, 0.0)
        intra = jnp.dot(att.astype(BF16), vj, preferred_element_type=F32)
        g_cum = jnp.exp2(bc)
        qt = (qf * g_cum).astype(BF16)
        tot = bc[0:1] if reverse else bc[c - 1:c]
        kh = (kf * jnp.exp2(tot - bc)).astype(BF16)
        upd = lax.dot_general(vj, kh, tn, preferred_element_type=F32)
        g_tot = g_cum[0:1] if reverse else g_cum[c - 1:c]
        parts.append((intra, qt, g_tot, upd))
    st = st_ref[...]
    outs = [None] * nc
    for j in (range(nc - 1, -1, -1) if reverse else range(nc)):
        intra, qt, g_tot, upd = parts[j]
        outs[j] = intra + lax.dot_general(qt, st.astype(BF16), nt, preferred_element_type=F32)
        st = st * g_tot + upd
    st_ref[...] = st
    return jnp.concatenate(outs, axis=0)


def _hgrn_kernel(qf_ref, kf_ref, hf_ref, lf_ref, vf_ref, qb_ref, kb_ref, hb_ref, lb_ref, vb_ref,
                 dall_ref, mask_ref, of_ref, ob_ref, st_ref, *, ts):
    gr = HG_C * HG_NC
    ngr = ts // gr

    @pl.when(pl.program_id(2) == 0)
    def _():
        st_ref[...] = jnp.zeros_like(st_ref)

    def body(i, carry):
        rows = pl.ds(pl.multiple_of(i * gr, gr), gr)
        of_ref[rows, :] = _hgrn_group(qf_ref[rows, :], kf_ref[rows, :], hf_ref[rows, :], lf_ref[rows, :],
                                      vf_ref[rows, :], st_ref.at[0], dall_ref[0], mask_ref.at[0], False)
        rows = pl.ds(pl.multiple_of((ngr - 1 - i) * gr, gr), gr)
        ob_ref[rows, :] = _hgrn_group(qb_ref[rows, :], kb_ref[rows, :], hb_ref[rows, :], lb_ref[rows, :],
                                      vb_ref[rows, :], st_ref.at[1], dall_ref[1], mask_ref.at[1], True)
        return carry

    lax.fori_loop(0, ngr, body, 0)


def _hgrn(q, lfh, lfl, kk, v, n_seq, seq, ts):
    t = q.shape[0]
    nt = seq // ts
    assert seq % ts == 0 and ts % (HG_C * HG_NC) == 0
    dall, masks = _hgrn_constants()
    dall = jnp.asarray(dall, BF16)
    masks = jnp.asarray(masks, F32)
    fwd = lambda off: pl.BlockSpec((ts, HG_DK), lambda b, h, j: (b * nt + j, h + off))
    bwd = lambda off: pl.BlockSpec((ts, HG_DK), lambda b, h, j: (b * nt + (nt - 1 - j), h + off))
    out = jax.ShapeDtypeStruct((t, HG_WIDTH), F32)
    return pl.pallas_call(
        functools.partial(_hgrn_kernel, ts=ts),
        grid=(n_seq, HG_HEADS, nt),
        in_specs=[fwd(0), fwd(0), fwd(0), fwd(0), fwd(0),
                  bwd(0), bwd(HG_HEADS), bwd(HG_HEADS), bwd(HG_HEADS), bwd(0),
                  _const_spec(dall.shape), _const_spec(masks.shape)],
        out_specs=[fwd(0), bwd(0)],
        out_shape=[out, out],
        scratch_shapes=[pltpu.VMEM((2, HG_DV, HG_DK), F32)],
        compiler_params=_cparams(("parallel", "parallel", "arbitrary")),
        name="hgrn2",
    )(q, kk, lfh, lfl, v, q, kk, lfh, lfl, v, dall, masks)


def _s5_matrices(lam_re, lam_im, log_dt, b_re, b_im, c_re, c_im, d_skip):
    hp = lax.Precision.HIGHEST
    ell, g, c, p = S5_L, S5_GROUPS, S5_GROUP, S5_STATE
    no, gpo = S5_OCTETS, S5_GPO
    dt = jnp.exp(log_dt)[..., None]

    def cpow(j):
        j = jnp.asarray(j, F32).reshape((-1, 1, 1, 1))
        mag = jnp.exp(lam_re * dt * j)
        ang = lam_im * dt * j
        return mag * jnp.cos(ang), mag * jnp.sin(ang)

    pr, pi = cpow(np.arange(ell + 1))
    abar_re, abar_im = pr[1], pi[1]
    den = lam_re * lam_re + lam_im * lam_im
    nr = abar_re - 1.0
    ni = abar_im
    cr = ((nr * lam_re + ni * lam_im) / den)[..., None]
    ci = ((ni * lam_re - nr * lam_im) / den)[..., None]
    bb_re = jnp.swapaxes(cr * b_re - ci * b_im, -1, -2)
    bb_im = jnp.swapaxes(cr * b_im + ci * b_re, -1, -2)

    prb, pib = pr[:, :, :, None, :], pi[:, :, :, None, :]
    wr = prb * bb_re - pib * bb_im
    wi = prb * bb_im + pib * bb_re
    kj = jnp.einsum('dgcp,jdgep->jdgce', jnp.concatenate([c_re, -c_im], axis=-1),
                    jnp.concatenate([wr[:ell], wi[:ell]], axis=-1), precision=hp)
    tau = np.arange(ell)

    skip = jnp.eye(c, dtype=F32) * d_skip.reshape(g, c, 1)
    klag = jnp.concatenate([kj[ell - 1:0:-1, 1], (kj[0, 0] + kj[0, 1] + skip)[None], kj[1:ell, 0]], axis=0)
    lag = tau[None, :] - tau[:, None] + (ell - 1)
    tz = klag[lag].transpose(2, 0, 4, 1, 3).reshape(g, S5_CW, S5_CW)

    mf_re, mf_im = wr[ell - 1 - tau, 0], wi[ell - 1 - tau, 0]
    mb_re, mb_im = wr[tau, 1], wi[tau, 1]
    mv = jnp.stack([jnp.stack([mf_re, mb_re], axis=-2), jnp.stack([mf_im, mb_im], axis=-2)], axis=-3)
    mv = jnp.swapaxes(mv, 0, 1).reshape(g, S5_CW, S5_GW)

    def cw(powers, d):
        p_re = pr[powers, d][:, :, None, :]
        p_im = pi[powers, d][:, :, None, :]
        return c_re[d] * p_re - c_im[d] * p_im, c_re[d] * p_im + c_im[d] * p_re

    nf_re, nf_im = cw(tau + 1, 0)
    nb_re, nb_im = cw(ell - tau, 1)
    nm = jnp.stack([jnp.stack([nf_re, nb_re], axis=-2), jnp.stack([-nf_im, -nb_im], axis=-2)], axis=-3)
    nm = jnp.swapaxes(jnp.swapaxes(nm, 0, 1).reshape(g, S5_CW, S5_GW), 1, 2)

    a_re = jnp.concatenate([pr[ell, 0], pr[ell, 1]], axis=-1).reshape(no, gpo * 2 * p)
    a_im = jnp.concatenate([pi[ell, 0], pi[ell, 1]], axis=-1).reshape(no, gpo * 2 * p)
    a16 = jnp.stack([a_re, a_im], axis=1)
    return tz.astype(BF16), mv.astype(BF16), nm.astype(BF16), a16.astype(F32)


def _s5_state_kernel(u_ref, mv_ref, a_ref, x_ref, v_scr, *, nseq, nchunk, segmented):
    hw = S5_SW // 2
    ntile = S5_SW // LANES
    gpo = S5_GPO

    def ld(r):
        return jnp.concatenate([v_scr[k, r, :] for k in range(ntile)], axis=1)

    def st(r, val):
        for k in range(ntile):
            v_scr[k, r, :] = val[:, k * LANES:(k + 1) * LANES]

    for i in range(gpo):
        inc = jnp.dot(u_ref[i], mv_ref[i], preferred_element_type=F32)
        for s in range(nseq):
            dst = pl.ds(s, nchunk, stride=nseq)
            v_scr[i, dst, :] = inc[s * nchunk:(s + 1) * nchunk, :LANES]
            v_scr[gpo + i, dst, :] = inc[s * nchunk:(s + 1) * nchunk, LANES:]
    a = a_ref[0]
    ar = jnp.broadcast_to(a[0:1], (nseq, hw))
    ai = jnp.broadcast_to(a[1:2], (nseq, hw))
    lane = lax.broadcasted_iota(jnp.int32, (nseq, S5_SW), 1)
    is_fwd2 = (lane & (2 * S5_STATE - 1)) < S5_STATE
    is_fwd = is_fwd2[:, :hw]

    def rows(i):
        rf = pl.ds(pl.multiple_of(i * nseq, nseq), nseq)
        rb = pl.ds(pl.multiple_of((nchunk - 1 - i) * nseq, nseq), nseq)
        return rf, rb

    def body(i, carry):
        xr, xi = carry
        rf, rb = rows(i)
        vf = ld(rf)
        vb = ld(rb)
        x = jnp.concatenate([xr, xi], axis=1)
        st(rf, jnp.where(is_fwd2, x, vf))
        st(rb, jnp.where(is_fwd2, vb, x))
        vr = jnp.where(is_fwd, vf[:, :hw], vb[:, :hw])
        vi = jnp.where(is_fwd, vf[:, hw:], vb[:, hw:])
        return ar * xr - ai * xi + vr, ar * xi + ai * xr + vi

    zero = jnp.zeros((nseq, hw), F32)
    er, ei = lax.fori_loop(0, nchunk, body, (zero, zero))

    if segmented:
        sr, si = a[0:1], a[1:2]
        for _ in range(nchunk.bit_length() - 1):
            sr, si = sr * sr - si * si, 2.0 * sr * si
        z1 = jnp.zeros((1, hw), F32)
        cf = [(z1, z1)]
        for s in range(nseq - 1):
            pr_, pi_ = cf[-1]
            cf.append((er[s:s + 1] + sr * pr_ - si * pi_, ei[s:s + 1] + sr * pi_ + si * pr_))
        cb = [(z1, z1)]
        for s in range(nseq - 1, 0, -1):
            pr_, pi_ = cb[0]
            cb.insert(0, (er[s:s + 1] + sr * pr_ - si * pi_, ei[s:s + 1] + sr * pi_ + si * pr_))
        cr = jnp.where(is_fwd, jnp.concatenate([t[0] for t in cf], axis=0),
                       jnp.concatenate([t[0] for t in cb], axis=0))
        ci = jnp.where(is_fwd, jnp.concatenate([t[1] for t in cf], axis=0),
                       jnp.concatenate([t[1] for t in cb], axis=0))

        def fix(i, carry):
            cr, ci = carry
            rf, rb = rows(i)
            x = jnp.concatenate([cr, ci], axis=1)
            st(rf, ld(rf) + jnp.where(is_fwd2, x, 0.0))
            st(rb, ld(rb) + jnp.where(is_fwd2, 0.0, x))
            return ar * cr - ai * ci, ar * ci + ai * cr

        lax.fori_loop(0, nchunk, fix, (cr, ci))

    for i in range(gpo):
        for s in range(nseq):
            src = pl.ds(s, nchunk, stride=nseq)
            x_ref[i, s * nchunk:(s + 1) * nchunk, :] = jnp.concatenate(
                [v_scr[i, src, :], v_scr[gpo + i, src, :]], axis=1).astype(BF16)


def _s5_out_kernel(u_ref, x_ref, tz_ref, nm_ref, y_ref):
    for i in range(S5_GPO):
        y = jnp.dot(u_ref[i], tz_ref[i], preferred_element_type=F32)
        y = y + jnp.dot(x_ref[i], nm_ref[i], preferred_element_type=F32)
        y_ref[i] = y.astype(BF16)


def _s5(ug, n_seq, seq, mats, tm_out):
    tz, mv, nm, a16 = mats
    nrow = ug.shape[1]
    nseq = S5_ROWS
    segmented = n_seq == 1
    if segmented:
        n_seq, seq = nseq, seq // nseq
    assert n_seq % nseq == 0 and seq % S5_L == 0
    nchunk = seq // S5_L
    assert nchunk & (nchunk - 1) == 0 and nchunk % 16 == 0
    m = nchunk * nseq
    blk = lambda r: pl.BlockSpec((S5_GPO, r, S5_CW), lambda o, i: (o, i, 0))
    mat = lambda r, c: pl.BlockSpec((S5_GPO, r, c), lambda o, i: (o, 0, 0))
    xin = pl.pallas_call(
        functools.partial(_s5_state_kernel, nseq=nseq, nchunk=nchunk, segmented=segmented),
        grid=(S5_OCTETS, nrow // m),
        in_specs=[blk(m), mat(S5_CW, S5_GW),
                  pl.BlockSpec((1, 2, S5_SW // 2), lambda o, i: (o, 0, 0))],
        out_specs=blk(m),
        out_shape=jax.ShapeDtypeStruct((S5_GROUPS, nrow, S5_GW), BF16),
        scratch_shapes=[pltpu.VMEM((S5_SW // LANES, m, LANES), F32)],
        compiler_params=_cparams(("parallel", "arbitrary")),
        name="s5_state",
    )(ug, mv, a16)
    return pl.pallas_call(
        _s5_out_kernel,
        grid=(S5_OCTETS, nrow // tm_out),
        in_specs=[blk(tm_out), blk(tm_out), mat(S5_CW, S5_CW), mat(S5_GW, S5_CW)],
        out_specs=blk(tm_out),
        out_shape=jax.ShapeDtypeStruct((S5_GROUPS, nrow, S5_CW), BF16),
        compiler_params=_cparams(("parallel", "arbitrary")),
        name="s5_out",
    )(ug, xin, tz, nm)


def _rms(x, g):
    return x * lax.rsqrt(jnp.mean(x * x, axis=-1, keepdims=True) + EPS) * g


def _out_kernel(x_ref, of_ref, ob_ref, g_ref, y2_ref, onorm_ref, gluw_ref, glub_ref,
                merge_ref, wout_ref, n2_ref, w1_ref, w2_ref, nf_ref, out_ref, y_scr):
    o = of_ref[...] + ob_ref[...]
    heads = []
    for h in range(HG_HEADS):
        oh = o[:, h * HG_DV:(h + 1) * HG_DV]
        heads.append(oh * lax.rsqrt(jnp.mean(oh * oh, axis=-1, keepdims=True) + EPS))
    g = g_ref[...].astype(F32)
    ya = jnp.concatenate(heads, axis=1) * onorm_ref[...] * (g * jax.nn.sigmoid(g))

    nrow = y_scr.shape[1] // S5_L
    seg = lax.broadcasted_iota(jnp.int32, (nrow, LANES), 1) // S5_GROUP
    for oc in range(S5_OCTETS):
        for half in range(S5_L // S5_GPO):
            src = [y2_ref[oc * S5_GPO + i, :, half * LANES:(half + 1) * LANES].astype(F32)
                   for i in range(S5_GPO)]
            for t8 in range(S5_GPO):
                y_scr[oc, pl.ds(half * S5_GPO + t8, nrow, stride=S5_L), :] = _regroup(src, t8, seg)
    z = jax.nn.gelu(jnp.concatenate([y_scr[oc] for oc in range(S5_OCTETS)], axis=1))
    gate = jnp.dot(z.astype(BF16), gluw_ref[...], preferred_element_type=F32) + glub_ref[...]
    z = z * jax.nn.sigmoid(gate)
    yb = _rms(z, merge_ref[...])

    ycat = jnp.concatenate([ya, yb], axis=1).astype(BF16)
    x1 = x_ref[...] + jnp.dot(ycat, wout_ref[...], preferred_element_type=F32)
    h2 = _rms(x1, n2_ref[...]).astype(BF16)
    acc = x1
    step = D_FF // 4
    for c in range(4):
        hid = jnp.dot(h2, w1_ref[:, c * step:(c + 1) * step], preferred_element_type=F32)
        hid = jnp.square(jnp.maximum(hid, 0.0)).astype(BF16)
        acc = acc + jnp.dot(hid, w2_ref[c * step:(c + 1) * step, :], preferred_element_type=F32)
    out_ref[...] = _rms(acc, nf_ref[...])


def _outstage(x, o_f, o_b, g, y2, onorm, gluw, glub, merge, wout, n2, w1, w2, nf, tm):
    t = x.shape[0]
    row = lambda w: pl.BlockSpec((tm, w), lambda i: (i, 0))
    vec = lambda w: _const_spec((1, w))
    return pl.pallas_call(
        _out_kernel,
        grid=(t // tm,),
        in_specs=[row(D_MODEL), row(HG_WIDTH), row(HG_WIDTH), row(HG_WIDTH),
                  pl.BlockSpec((S5_GROUPS, tm // S5_L, S5_CW), lambda i: (0, i, 0)),
                  vec(HG_WIDTH), _const_spec((S5_WIDTH, S5_WIDTH)), vec(S5_WIDTH),
                  vec(S5_WIDTH), _const_spec((D_MODEL, D_MODEL)), vec(D_MODEL),
                  _const_spec((D_MODEL, D_FF)), _const_spec((D_FF, D_MODEL)), vec(D_MODEL)],
        out_specs=row(D_MODEL),
        out_shape=jax.ShapeDtypeStruct((t, D_MODEL), F32),
        scratch_shapes=[pltpu.VMEM((S5_OCTETS, tm, LANES), F32)],
        compiler_params=_cparams(("parallel",)),
        name="outstage",
    )(x, o_f, o_b, g, y2, onorm, gluw, glub, merge, wout, n2, w1, w2, nf)


HG_TILE = 2048
ROW_TILE_IN = 512
ROW_TILE_OUT = 512
S5_TILE_OUT = 512


def _trunk(x3, p):
    b, s, dm = x3.shape
    x = x3.reshape(b * s, dm)
    q, lfh, lfl, kk, v, g, u2 = _inproj(x, p["g1"], p["w_in"], p["lb"], ROW_TILE_IN)
    o_f, o_b = _hgrn(q, lfh, lfl, kk, v, b, s, min(HG_TILE, s))
    y2 = _s5(u2, b, s, p["s5"], min(S5_TILE_OUT, b * s // S5_L))
    out = _outstage(x, o_f, o_b, g, y2, p["onorm"], p["gluw"], p["glub"], p["merge"],
                    p["wout"], p["n2"], p["w1"], p["w2"], p["nf"], ROW_TILE_OUT)
    return out.reshape(b, s, dm)


def kernel(x_prompt, x_sample, norm1_g, w_in, hgrn_lb, hgrn_onorm_g, s5_lambda_re, s5_lambda_im,
           s5_log_dt, s5_b_re, s5_b_im, s5_c_re, s5_c_im, s5_d, s5_glu_w, s5_glu_b, s5_merge_g,
           w_out, norm2_g, w_ff1, w_ff2, norm_f_g):
    l = 0
    lbs = jnp.cumsum(jax.nn.softmax(hgrn_lb.astype(F32), axis=1), axis=1)[:, l]
    row = lambda a: a.astype(F32).reshape(1, -1)
    f = lambda a: a[l].astype(F32)
    p = {
        "g1": row(norm1_g[l]),
        "w_in": w_in[l].astype(BF16),
        "lb": lbs.reshape(1, 2 * HG_FDIM),
        "s5": _s5_matrices(f(s5_lambda_re), f(s5_lambda_im), f(s5_log_dt), f(s5_b_re), f(s5_b_im),
                           f(s5_c_re), f(s5_c_im), f(s5_d)),
        "onorm": row(hgrn_onorm_g[l]),
        "gluw": s5_glu_w[l].astype(BF16),
        "glub": row(s5_glu_b[l]),
        "merge": row(s5_merge_g[l]),
        "wout": w_out[l].astype(BF16),
        "n2": row(norm2_g[l]),
        "w1": w_ff1[l].astype(BF16),
        "w2": w_ff2[l].astype(BF16),
        "nf": row(norm_f_g),
    }
    return (_trunk(x_prompt, p), _trunk(x_sample, p))
```

```python
import functools

import numpy as np
import jax
import jax.numpy as jnp
from jax import lax
from jax.experimental import pallas as pl
from jax.experimental.pallas import tpu as pltpu

F32 = jnp.float32
BF16 = jnp.bfloat16

D_MODEL = 1024
HG_HEADS = 4
HG_DK = 128
HG_DV = 128
HG_FDIM = HG_HEADS * HG_DK
HG_WIDTH = HG_HEADS * HG_DV
S5_WIDTH = 512
S5_GROUP = 16
S5_GROUPS = S5_WIDTH // S5_GROUP
S5_STATE = 64
D_FF = 4 * D_MODEL
IN_WIDTH = 3 * HG_FDIM + 2 * HG_WIDTH + S5_WIDTH
EPS = 1e-6

LANES = 128
VMEM_LIMIT_BYTES = 56 * 1024 * 1024

ROW_TILE_IN = 512
ROW_TILE_OUT = 512
HG_TILE = 2048
S5_TILE_OUT = 512

HG_C = 64
HG_LEVELS = (32, 16, 8, 4, 2, 1)
HG_SUB = 8
HG_NBLK = 1 + sum(m < HG_SUB for m in HG_LEVELS)

S5_L = 16
S5_OCTETS = S5_WIDTH // LANES
S5_GPO = LANES // S5_GROUP
S5_CW = S5_L * S5_GROUP
S5_GW = 4 * S5_STATE
S5_SW = S5_GW * S5_GPO
S5_ROWS = 8


def _cparams(sem):
    return pltpu.CompilerParams(dimension_semantics=sem, vmem_limit_bytes=VMEM_LIMIT_BYTES)


def _const_spec(shape):
    n = len(shape)
    return pl.BlockSpec(shape, lambda *_: (0,) * n, pipeline_mode=pl.Buffered(1))


def _regroup(srcs, sel, seg):
    out = None
    for k, s in enumerate(srcs):
        shift = ((k - sel) % S5_GPO) * S5_GROUP
        r = s if shift == 0 else pltpu.roll(s, shift, axis=1)
        out = r if out is None else jnp.where(seg == k, r, out)
    return out


def _inproj_kernel(x_ref, g1_ref, w_ref, lb_ref, q_ref, lfh_ref, lfl_ref, kk_ref, v_ref, g_ref, u_ref,
                   u_scr):
    x = x_ref[...]
    ms = jnp.mean(x * x, axis=-1, keepdims=True)
    h = (x * lax.rsqrt(ms + EPS) * g1_ref[...]).astype(BF16)
    o1 = HG_FDIM
    o2 = 3 * HG_FDIM
    o3 = o2 + HG_WIDTH
    o4 = o3 + HG_WIDTH
    proj = lambda lo, hi: jnp.dot(h, w_ref[:, lo:hi], preferred_element_type=F32)
    nrow = u_scr.shape[1] // S5_L
    seg = lax.broadcasted_iota(jnp.int32, (nrow, LANES), 1) // S5_GROUP
    pu = proj(o4, IN_WIDTH)
    for o in range(S5_OCTETS):
        u_scr[o] = pu[:, o * LANES:(o + 1) * LANES]
        src = [u_scr[o, pl.ds(tau, nrow, stride=S5_L), :] for tau in range(S5_L)]
        for half in range(S5_L // S5_GPO):
            for i in range(S5_GPO):
                u_ref[o * S5_GPO + i, :, half * LANES:(half + 1) * LANES] = _regroup(
                    src[half * S5_GPO:(half + 1) * S5_GPO], i, seg).astype(BF16)
    for d in range(2):
        z = proj(o1 + d * HG_FDIM, o1 + (d + 1) * HG_FDIM)
        cols = slice(d * HG_FDIM, (d + 1) * HG_FDIM)
        lb = lb_ref[:, cols]
        sg = jax.nn.sigmoid(z)
        f = lb + (1.0 - lb) * sg
        lf = jnp.log2(f)
        lfh = lf.astype(BF16)
        lfh_ref[:, cols] = lfh
        lfl_ref[:, cols] = (lf - lfh.astype(F32)).astype(BF16)
        kk_ref[:, cols] = ((1.0 - lb) * (1.0 - sg)).astype(BF16)
    q_ref[...] = proj(0, o1).astype(BF16)
    v_ref[...] = proj(o2, o3).astype(BF16)
    g_ref[...] = proj(o3, o4).astype(BF16)


def _inproj(x, g1, w_in, lb, tm):
    t = x.shape[0]
    row = lambda w: pl.BlockSpec((tm, w), lambda i: (i, 0))
    outs = [
        jax.ShapeDtypeStruct((t, HG_FDIM), BF16),
        jax.ShapeDtypeStruct((t, 2 * HG_FDIM), BF16),
        jax.ShapeDtypeStruct((t, 2 * HG_FDIM), BF16),
        jax.ShapeDtypeStruct((t, 2 * HG_FDIM), BF16),
        jax.ShapeDtypeStruct((t, HG_WIDTH), BF16),
        jax.ShapeDtypeStruct((t, HG_WIDTH), BF16),
        jax.ShapeDtypeStruct((S5_GROUPS, t // S5_L, S5_CW), BF16),
    ]
    return pl.pallas_call(
        _inproj_kernel,
        grid=(t // tm,),
        in_specs=[row(D_MODEL), _const_spec((1, D_MODEL)), _const_spec((D_MODEL, IN_WIDTH)),
                  _const_spec((1, 2 * HG_FDIM))],
        out_specs=[row(HG_FDIM), row(2 * HG_FDIM), row(2 * HG_FDIM), row(2 * HG_FDIM),
                   row(HG_WIDTH), row(HG_WIDTH),
                   pl.BlockSpec((S5_GROUPS, tm // S5_L, S5_CW), lambda i: (0, i, 0))],
        out_shape=outs,
        scratch_shapes=[pltpu.VMEM((S5_OCTETS, tm, LANES), F32)],
        compiler_params=_cparams(("parallel",)),
        name="inproj",
    )(x, g1, w_in, lb)


def _hgrn_constants():
    c = HG_C
    idx = np.arange(c)
    blocks = [(idx[None, :] <= idx[:, None])]
    masks = []
    for m in HG_LEVELS:
        rho = (idx // (2 * m)) * (2 * m) + m
        late = idx >= rho
        x = idx[None, :]
        r = idx[:, None]
        d = np.where(late[:, None], (x >= rho[:, None]) & (x <= r), (x > r) & (x < rho[:, None]))
        if m < HG_SUB:
            blocks.append(d)
        same = (idx[:, None] // (2 * m)) == (idx[None, :] // (2 * m))
        masks.append(same & late[:, None] & (~late)[None, :])
    d_fwd = np.concatenate(blocks, axis=0).astype(np.float32)
    d_bwd = np.concatenate([b[::-1, ::-1] for b in blocks], axis=0).astype(np.float32)
    m_fwd = np.stack(masks).astype(np.float32)
    m_bwd = m_fwd[:, ::-1, ::-1]
    dall = np.stack([d_fwd, d_bwd])
    dall = np.concatenate([dall, dall], axis=-1)
    return dall, np.stack([m_fwd, m_bwd])


def _hgrn_group(q, k, lfh, lfl, v, st_ref, dall, mask_ref, reverse):
    c = HG_C
    nc = q.shape[0] // c
    nt = (((1,), (1,)), ((), ()))
    tn = (((0,), (0,)), ((), ()))
    sl = lambda a, j: a[j * c:(j + 1) * c]
    lf2 = jnp.concatenate(
        [jnp.concatenate([sl(lfh, j), sl(lfl, j)], axis=0) for j in range(nc)], axis=1)
    e_all = jnp.dot(dall, lf2, preferred_element_type=F32)
    row = lax.broadcasted_iota(jnp.int32, (c, HG_DK), 0)
    if reverse:
        row = (c - 1) - row
    sub = HG_SUB
    ones = jnp.ones((HG_DK, c), BF16)
    eye = lax.broadcasted_iota(jnp.int32, (c, c), 0) == lax.broadcasted_iota(jnp.int32, (c, c), 1)
    dg_all = jnp.dot(q * k, ones, preferred_element_type=F32)
    parts = []
    for j in range(nc):
        ej = e_all[:, j * HG_DK:(j + 1) * HG_DK]
        bc = ej[0:c]
        qf = sl(q, j).astype(F32)
        kf = sl(k, j).astype(F32)
        vj = sl(v, j)
        att = [jnp.zeros((sub, c), F32) for _ in range(c // sub)]
        for li, m in enumerate(HG_LEVELS):
            msk = mask_ref[li]
            if m >= sub:
                pieces = []
                for pb in range(c // (2 * m)):
                    lo, mid, hi = pb * 2 * m, pb * 2 * m + m, (pb + 1) * 2 * m
                    ref = bc[mid:mid + 1] if reverse else bc[mid - 1:mid]
                    first, second = bc[lo:mid], bc[mid:hi]
                    pieces += ([first - ref, ref - second] if reverse else [ref - first, second - ref])
                g_l = jnp.exp2(jnp.concatenate(pieces, axis=0))
                late = [b for b in range(c // m) if ((c // m - 1 - b) if reverse else b) % 2 == 1]
                qk = jnp.concatenate(
                    [(qf if b in late else kf)[b * m:(b + 1) * m] for b in range(c // m)], axis=0)
                xl = (qk * g_l).astype(BF16)
                xq = jnp.concatenate([xl[b * m:(b + 1) * m] for b in late], axis=0)
                a_l = lax.dot_general(xq, xl, nt, preferred_element_type=F32)
                for n, b in enumerate(late):
                    for r in range(m // sub):
                        t = b * (m // sub) + r
                        lo = n * m + r * sub
                        att[t] = att[t] + a_l[lo:lo + sub] * msk[t * sub:(t + 1) * sub]
            else:
                nf = 1 + li - sum(mm >= sub for mm in HG_LEVELS)
                g_l = jnp.exp2(ej[nf * c:(nf + 1) * c])
                xl = (jnp.where((row & (2 * m - 1)) >= m, qf, kf) * g_l).astype(BF16)
                a_l = lax.dot_general(xl, xl, nt, preferred_element_type=F32)
                for t in range(c // sub):
                    att[t] = att[t] + a_l[t * sub:(t + 1) * sub] * msk[t * sub:(t + 1) * sub]
        att = jnp.concatenate(att, axis=0) + jnp.where(eye, sl(dg_all, j), 0.0)
        intra = jnp.dot(att.astype(BF16), vj, preferred_element_type=F32)
        g_cum = jnp.exp2(bc)
        qt = (qf * g_cum).astype(BF16)
        tot = bc[0:1] if reverse else bc[c - 1:c]
        kh = (kf * jnp.exp2(tot - bc)).astype(BF16)
        upd = lax.dot_general(kh, vj, tn, preferred_element_type=F32)
        g_tot = g_cum[0:1] if reverse else g_cum[c - 1:c]
        parts.append((intra, qt, g_tot, upd))
    st = st_ref[...]
    outs = [None] * nc
    for j in (range(nc - 1, -1, -1) if reverse else range(nc)):
        intra, qt, g_tot, upd = parts[j]
        outs[j] = intra + jnp.dot(qt, st.astype(BF16), preferred_element_type=F32)
        st = st * jnp.transpose(jnp.broadcast_to(g_tot, (HG_DV, HG_DK))) + upd
    st_ref[...] = st
    return jnp.concatenate(outs, axis=0)


def _hgrn_kernel(qf_ref, kf_ref, hf_ref, lf_ref, vf_ref, qb_ref, kb_ref, hb_ref, lb_ref, vb_ref,
                 dall_ref, mask_ref, of_ref, ob_ref, st_ref):
    @pl.when(pl.program_id(2) == 0)
    def _():
        st_ref[...] = jnp.zeros_like(st_ref)

    of_ref[...] = _hgrn_group(qf_ref[...], kf_ref[...], hf_ref[...], lf_ref[...], vf_ref[...],
                              st_ref.at[0], dall_ref[0], mask_ref.at[0], False)
    ob_ref[...] = _hgrn_group(qb_ref[...], kb_ref[...], hb_ref[...], lb_ref[...], vb_ref[...],
                              st_ref.at[1], dall_ref[1], mask_ref.at[1], True)


def _hgrn(q, lfh, lfl, kk, v, n_seq, seq, ts):
    t = q.shape[0]
    nt = seq // ts
    assert seq % ts == 0 and ts % HG_C == 0
    dall, masks = _hgrn_constants()
    dall = jnp.asarray(dall, BF16)
    masks = jnp.asarray(masks, F32)
    fwd = lambda off: pl.BlockSpec((ts, HG_DK), lambda b, h, j: (b * nt + j, h + off))
    bwd = lambda off: pl.BlockSpec((ts, HG_DK), lambda b, h, j: (b * nt + (nt - 1 - j), h + off))
    out = jax.ShapeDtypeStruct((t, HG_WIDTH), F32)
    return pl.pallas_call(
        _hgrn_kernel,
        grid=(n_seq, HG_HEADS, nt),
        in_specs=[fwd(0), fwd(0), fwd(0), fwd(0), fwd(0),
                  bwd(0), bwd(HG_HEADS), bwd(HG_HEADS), bwd(HG_HEADS), bwd(0),
                  _const_spec(dall.shape), _const_spec(masks.shape)],
        out_specs=[fwd(0), bwd(0)],
        out_shape=[out, out],
        scratch_shapes=[pltpu.VMEM((2, HG_DK, HG_DV), F32)],
        compiler_params=_cparams(("parallel", "parallel", "arbitrary")),
        name="hgrn2",
    )(q, kk, lfh, lfl, v, q, kk, lfh, lfl, v, dall, masks)


def _s5_matrices(lam_re, lam_im, log_dt, b_re, b_im, c_re, c_im, d_skip):
    hp = lax.Precision.HIGHEST
    ell, g, c, p = S5_L, S5_GROUPS, S5_GROUP, S5_STATE
    no, gpo = S5_OCTETS, S5_GPO
    dt = jnp.exp(log_dt)[..., None]

    def cpow(j):
        j = jnp.asarray(j, F32).reshape((-1, 1, 1, 1))
        mag = jnp.exp(lam_re * dt * j)
        ang = lam_im * dt * j
        return mag * jnp.cos(ang), mag * jnp.sin(ang)

    pr, pi = cpow(np.arange(ell + 1))
    abar_re, abar_im = pr[1], pi[1]
    den = lam_re * lam_re + lam_im * lam_im
    nr = abar_re - 1.0
    ni = abar_im
    cr = ((nr * lam_re + ni * lam_im) / den)[..., None]
    ci = ((ni * lam_re - nr * lam_im) / den)[..., None]
    bb_re = jnp.swapaxes(cr * b_re - ci * b_im, -1, -2)
    bb_im = jnp.swapaxes(cr * b_im + ci * b_re, -1, -2)

    prb, pib = pr[:, :, :, None, :], pi[:, :, :, None, :]
    wr = prb * bb_re - pib * bb_im
    wi = prb * bb_im + pib * bb_re
    kj = jnp.einsum('dgcp,jdgep->jdgce', jnp.concatenate([c_re, -c_im], axis=-1),
                    jnp.concatenate([wr[:ell], wi[:ell]], axis=-1), precision=hp)
    tau = np.arange(ell)

    skip = jnp.eye(c, dtype=F32) * d_skip.reshape(g, c, 1)
    klag = jnp.concatenate([kj[ell - 1:0:-1, 1], (kj[0, 0] + kj[0, 1] + skip)[None], kj[1:ell, 0]], axis=0)
    lag = tau[None, :] - tau[:, None] + (ell - 1)
    tz = klag[lag].transpose(2, 0, 4, 1, 3).reshape(g, S5_CW, S5_CW)

    mf_re, mf_im = wr[ell - 1 - tau, 0], wi[ell - 1 - tau, 0]
    mb_re, mb_im = wr[tau, 1], wi[tau, 1]
    mv = jnp.concatenate([mf_re, mb_re, mf_im, mb_im], axis=-1)
    mv = jnp.swapaxes(mv, 0, 1).reshape(g, S5_CW, S5_GW)

    def cw(powers, d):
        p_re = pr[powers, d][:, :, None, :]
        p_im = pi[powers, d][:, :, None, :]
        return c_re[d] * p_re - c_im[d] * p_im, c_re[d] * p_im + c_im[d] * p_re

    nf_re, nf_im = cw(tau + 1, 0)
    nb_re, nb_im = cw(ell - tau, 1)
    nm = jnp.concatenate([nf_re, nb_re, -nf_im, -nb_im], axis=-1)
    nm = jnp.swapaxes(jnp.swapaxes(nm, 0, 1).reshape(g, S5_CW, S5_GW), 1, 2)

    a_re = jnp.concatenate([pr[ell, 0], pr[ell, 1]], axis=-1).reshape(no, gpo * 2 * p)
    a_im = jnp.concatenate([pi[ell, 0], pi[ell, 1]], axis=-1).reshape(no, gpo * 2 * p)
    a16 = jnp.stack([a_re, a_im], axis=1)
    return tz.astype(BF16), mv.astype(BF16), nm.astype(BF16), a16.astype(F32)


def _s5_state_kernel(u_ref, mv_ref, a_ref, x_ref, v_scr, *, nseq, nchunk, segmented):
    hw = S5_SW // 2
    ntile = S5_SW // LANES
    gpo = S5_GPO

    def ld(r):
        return jnp.concatenate([v_scr[k, r, :] for k in range(ntile)], axis=1)

    def st(r, val):
        for k in range(ntile):
            v_scr[k, r, :] = val[:, k * LANES:(k + 1) * LANES]

    for i in range(gpo):
        inc = jnp.dot(u_ref[i], mv_ref[i], preferred_element_type=F32)
        for s in range(nseq):
            dst = pl.ds(s, nchunk, stride=nseq)
            v_scr[i, dst, :] = inc[s * nchunk:(s + 1) * nchunk, :LANES]
            v_scr[gpo + i, dst, :] = inc[s * nchunk:(s + 1) * nchunk, LANES:]
    a = a_ref[0]
    ar = jnp.broadcast_to(a[0:1], (nseq, hw))
    ai = jnp.broadcast_to(a[1:2], (nseq, hw))
    lane = lax.broadcasted_iota(jnp.int32, (nseq, S5_SW), 1)
    is_fwd2 = (lane & (2 * S5_STATE - 1)) < S5_STATE
    is_fwd = is_fwd2[:, :hw]

    def rows(i):
        rf = pl.ds(pl.multiple_of(i * nseq, nseq), nseq)
        rb = pl.ds(pl.multiple_of((nchunk - 1 - i) * nseq, nseq), nseq)
        return rf, rb

    def body(i, carry):
        xr, xi = carry
        rf, rb = rows(i)
        vf = ld(rf)
        vb = ld(rb)
        x = jnp.concatenate([xr, xi], axis=1)
        st(rf, jnp.where(is_fwd2, x, vf))
        st(rb, jnp.where(is_fwd2, vb, x))
        vr = jnp.where(is_fwd, vf[:, :hw], vb[:, :hw])
        vi = jnp.where(is_fwd, vf[:, hw:], vb[:, hw:])
        return ar * xr - ai * xi + vr, ar * xi + ai * xr + vi

    zero = jnp.zeros((nseq, hw), F32)
    er, ei = lax.fori_loop(0, nchunk, body, (zero, zero))

    if segmented:
        sr, si = a[0:1], a[1:2]
        for _ in range(nchunk.bit_length() - 1):
            sr, si = sr * sr - si * si, 2.0 * sr * si
        z1 = jnp.zeros((1, hw), F32)
        cf = [(z1, z1)]
        for s in range(nseq - 1):
            pr_, pi_ = cf[-1]
            cf.append((er[s:s + 1] + sr * pr_ - si * pi_, ei[s:s + 1] + sr * pi_ + si * pr_))
        cb = [(z1, z1)]
        for s in range(nseq - 1, 0, -1):
            pr_, pi_ = cb[0]
            cb.insert(0, (er[s:s + 1] + sr * pr_ - si * pi_, ei[s:s + 1] + sr * pi_ + si * pr_))
        cr = jnp.where(is_fwd, jnp.concatenate([t[0] for t in cf], axis=0),
                       jnp.concatenate([t[0] for t in cb], axis=0))
        ci = jnp.where(is_fwd, jnp.concatenate([t[1] for t in cf], axis=0),
                       jnp.concatenate([t[1] for t in cb], axis=0))

        def fix(i, carry):
            cr, ci = carry
            rf, rb = rows(i)
            x = jnp.concatenate([cr, ci], axis=1)
            st(rf, ld(rf) + jnp.where(is_fwd2, x, 0.0))
            st(rb, ld(rb) + jnp.where(is_fwd2, 0.0, x))
            return ar * cr - ai * ci, ar * ci + ai * cr

        lax.fori_loop(0, nchunk, fix, (cr, ci))

    for i in range(gpo):
        for s in range(nseq):
            src = pl.ds(s, nchunk, stride=nseq)
            x_ref[i, s * nchunk:(s + 1) * nchunk, :] = jnp.concatenate(
                [v_scr[i, src, :], v_scr[gpo + i, src, :]], axis=1).astype(BF16)


def _s5_out_kernel(u_ref, x_ref, tz_ref, nm_ref, y_ref):
    for i in range(S5_GPO):
        y = jnp.dot(u_ref[i], tz_ref[i], preferred_element_type=F32)
        y = y + jnp.dot(x_ref[i], nm_ref[i], preferred_element_type=F32)
        y_ref[i] = y.astype(BF16)


def _s5(ug, n_seq, seq, mats, tm_out):
    tz, mv, nm, a16 = mats
    nrow = ug.shape[1]
    nseq = S5_ROWS
    segmented = n_seq == 1
    if segmented:
        n_seq, seq = nseq, seq // nseq
    assert n_seq % nseq == 0 and seq % S5_L == 0
    nchunk = seq // S5_L
    assert nchunk & (nchunk - 1) == 0 and nchunk % 16 == 0
    m = nchunk * nseq
    blk = lambda r: pl.BlockSpec((S5_GPO, r, S5_CW), lambda o, i: (o, i, 0))
    mat = lambda r, c: pl.BlockSpec((S5_GPO, r, c), lambda o, i: (o, 0, 0))
    xin = pl.pallas_call(
        functools.partial(_s5_state_kernel, nseq=nseq, nchunk=nchunk, segmented=segmented),
        grid=(S5_OCTETS, nrow // m),
        in_specs=[blk(m), mat(S5_CW, S5_GW),
                  pl.BlockSpec((1, 2, S5_SW // 2), lambda o, i: (o, 0, 0))],
        out_specs=blk(m),
        out_shape=jax.ShapeDtypeStruct((S5_GROUPS, nrow, S5_GW), BF16),
        scratch_shapes=[pltpu.VMEM((S5_SW // LANES, m, LANES), F32)],
        compiler_params=_cparams(("parallel", "arbitrary")),
        name="s5_state",
    )(ug, mv, a16)
    return pl.pallas_call(
        _s5_out_kernel,
        grid=(S5_OCTETS, nrow // tm_out),
        in_specs=[blk(tm_out), blk(tm_out), mat(S5_CW, S5_CW), mat(S5_GW, S5_CW)],
        out_specs=blk(tm_out),
        out_shape=jax.ShapeDtypeStruct((S5_GROUPS, nrow, S5_CW), BF16),
        compiler_params=_cparams(("parallel", "arbitrary")),
        name="s5_out",
    )(ug, xin, tz, nm)


def _rms(x, g):
    return x * lax.rsqrt(jnp.mean(x * x, axis=-1, keepdims=True) + EPS) * g


def _out_kernel(x_ref, of_ref, ob_ref, g_ref, y2_ref, onorm_ref, gluw_ref, glub_ref,
                merge_ref, wout_ref, n2_ref, w1_ref, w2_ref, nf_ref, out_ref, y_scr):
    o = of_ref[...] + ob_ref[...]
    heads = []
    for h in range(HG_HEADS):
        oh = o[:, h * HG_DV:(h + 1) * HG_DV]
        heads.append(oh * lax.rsqrt(jnp.mean(oh * oh, axis=-1, keepdims=True) + EPS))
    g = g_ref[...].astype(F32)
    ya = jnp.concatenate(heads, axis=1) * onorm_ref[...] * (g * jax.nn.sigmoid(g))

    nrow = y_scr.shape[1] // S5_L
    seg = lax.broadcasted_iota(jnp.int32, (nrow, LANES), 1) // S5_GROUP
    for oc in range(S5_OCTETS):
        for half in range(S5_L // S5_GPO):
            src = [y2_ref[oc * S5_GPO + i, :, half * LANES:(half + 1) * LANES].astype(F32)
                   for i in range(S5_GPO)]
            for t8 in range(S5_GPO):
                y_scr[oc, pl.ds(half * S5_GPO + t8, nrow, stride=S5_L), :] = _regroup(src, t8, seg)
    z = jax.nn.gelu(jnp.concatenate([y_scr[oc] for oc in range(S5_OCTETS)], axis=1))
    gate = jnp.dot(z.astype(BF16), gluw_ref[...], preferred_element_type=F32) + glub_ref[...]
    z = z * jax.nn.sigmoid(gate)
    yb = _rms(z, merge_ref[...])

    ycat = jnp.concatenate([ya, yb], axis=1).astype(BF16)
    x1 = x_ref[...] + jnp.dot(ycat, wout_ref[...], preferred_element_type=F32)
    h2 = _rms(x1, n2_ref[...]).astype(BF16)
    acc = x1
    step = D_FF // 4
    for c in range(4):
        hid = jnp.dot(h2, w1_ref[:, c * step:(c + 1) * step], preferred_element_type=F32)
        hid = jnp.square(jnp.maximum(hid, 0.0)).astype(BF16)
        acc = acc + jnp.dot(hid, w2_ref[c * step:(c + 1) * step, :], preferred_element_type=F32)
    out_ref[...] = _rms(acc, nf_ref[...])


def _outstage(x, o_f, o_b, g, y2, onorm, gluw, glub, merge, wout, n2, w1, w2, nf, tm):
    t = x.shape[0]
    row = lambda w: pl.BlockSpec((tm, w), lambda i: (i, 0))
    vec = lambda w: _const_spec((1, w))
    return pl.pallas_call(
        _out_kernel,
        grid=(t // tm,),
        in_specs=[row(D_MODEL), row(HG_WIDTH), row(HG_WIDTH), row(HG_WIDTH),
                  pl.BlockSpec((S5_GROUPS, tm // S5_L, S5_CW), lambda i: (0, i, 0)),
                  vec(HG_WIDTH), _const_spec((S5_WIDTH, S5_WIDTH)), vec(S5_WIDTH),
                  vec(S5_WIDTH), _const_spec((D_MODEL, D_MODEL)), vec(D_MODEL),
                  _const_spec((D_MODEL, D_FF)), _const_spec((D_FF, D_MODEL)), vec(D_MODEL)],
        out_specs=row(D_MODEL),
        out_shape=jax.ShapeDtypeStruct((t, D_MODEL), F32),
        scratch_shapes=[pltpu.VMEM((S5_OCTETS, tm, LANES), F32)],
        compiler_params=_cparams(("parallel",)),
        name="outstage",
    )(x, o_f, o_b, g, y2, onorm, gluw, glub, merge, wout, n2, w1, w2, nf)


def _trunk(x3, p):
    b, s, dm = x3.shape
    x = x3.reshape(b * s, dm)
    q, lfh, lfl, kk, v, g, u2 = _inproj(x, p["g1"], p["w_in"], p["lb"], ROW_TILE_IN)
    o_f, o_b = _hgrn(q, lfh, lfl, kk, v, b, s, min(HG_TILE, s))
    y2 = _s5(u2, b, s, p["s5"], min(S5_TILE_OUT, b * s // S5_L))
    out = _outstage(x, o_f, o_b, g, y2, p["onorm"], p["gluw"], p["glub"], p["merge"],
                    p["wout"], p["n2"], p["w1"], p["w2"], p["nf"], ROW_TILE_OUT)
    return out.reshape(b, s, dm)


def kernel(x_prompt, x_sample, norm1_g, w_in, hgrn_lb, hgrn_onorm_g, s5_lambda_re, s5_lambda_im,
           s5_log_dt, s5_b_re, s5_b_im, s5_c_re, s5_c_im, s5_d, s5_glu_w, s5_glu_b, s5_merge_g,
           w_out, norm2_g, w_ff1, w_ff2, norm_f_g):
    l = 0
    lbs = jnp.cumsum(jax.nn.softmax(hgrn_lb.astype(F32), axis=1), axis=1)[:, l]
    row = lambda a: a.astype(F32).reshape(1, -1)
    f = lambda a: a[l].astype(F32)
    p = {
        "g1": row(norm1_g[l]),
        "w_in": w_in[l].astype(BF16),
        "lb": lbs.reshape(1, 2 * HG_FDIM),
        "s5": _s5_matrices(f(s5_lambda_re), f(s5_lambda_im), f(s5_log_dt), f(s5_b_re), f(s5_b_im),
                           f(s5_c_re), f(s5_c_im), f(s5_d)),
        "onorm": row(hgrn_onorm_g[l]),
        "gluw": s5_glu_w[l].astype(BF16),
        "glub": row(s5_glu_b[l]),
        "merge": row(s5_merge_g[l]),
        "wout": w_out[l].astype(BF16),
        "n2": row(norm2_g[l]),
        "w1": w_ff1[l].astype(BF16),
        "w2": w_ff2[l].astype(BF16),
        "nf": row(norm_f_g),
    }
    return (_trunk(x_prompt, p), _trunk(x_sample, p))
```

```python
import functools

import numpy as np
import jax
import jax.numpy as jnp
from jax import lax
from jax.experimental import pallas as pl
from jax.experimental.pallas import tpu as pltpu

F32 = jnp.float32
BF16 = jnp.bfloat16

D_MODEL = 1024
HG_HEADS = 4
HG_DK = 128
HG_DV = 128
HG_FDIM = HG_HEADS * HG_DK
HG_WIDTH = HG_HEADS * HG_DV
S5_WIDTH = 512
S5_GROUP = 16
S5_GROUPS = S5_WIDTH // S5_GROUP
S5_STATE = 64
D_FF = 4 * D_MODEL
IN_WIDTH = 3 * HG_FDIM + 2 * HG_WIDTH + S5_WIDTH
EPS = 1e-6

LANES = 128
VMEM_LIMIT_BYTES = 56 * 1024 * 1024

ROW_TILE_IN = 512
ROW_TILE_OUT = 512
HG_TILE = 2048
S5_TILE_OUT = 512

HG_C = 64
HG_LEVELS = (32, 16, 8, 4, 2, 1)
HG_SUB = 8
HG_NBLK = 1 + sum(1 < m < HG_SUB for m in HG_LEVELS)

S5_L = 16
S5_OCTETS = S5_WIDTH // LANES
S5_GPO = LANES // S5_GROUP
S5_CW = S5_L * S5_GROUP
S5_GW = 4 * S5_STATE
S5_SW = S5_GW * S5_GPO
S5_ROWS = 8


def _cparams(sem):
    return pltpu.CompilerParams(dimension_semantics=sem, vmem_limit_bytes=VMEM_LIMIT_BYTES)


def _const_spec(shape):
    n = len(shape)
    return pl.BlockSpec(shape, lambda *_: (0,) * n, pipeline_mode=pl.Buffered(1))


def _regroup(srcs, sel, seg):
    out = None
    for k, s in enumerate(srcs):
        shift = ((k - sel) % S5_GPO) * S5_GROUP
        r = s if shift == 0 else pltpu.roll(s, shift, axis=1)
        out = r if out is None else jnp.where(seg == k, r, out)
    return out


def _inproj_kernel(x_ref, g1_ref, w_ref, lb_ref, q_ref, lfh_ref, lfl_ref, kk_ref, v_ref, g_ref, u_ref,
                   u_scr):
    x = x_ref[...]
    ms = jnp.mean(x * x, axis=-1, keepdims=True)
    h = (x * lax.rsqrt(ms + EPS) * g1_ref[...]).astype(BF16)
    o1 = HG_FDIM
    o2 = 3 * HG_FDIM
    o3 = o2 + HG_WIDTH
    o4 = o3 + HG_WIDTH
    proj = lambda lo, hi: jnp.dot(h, w_ref[:, lo:hi], preferred_element_type=F32)
    for d in range(2):
        z = proj(o1 + d * HG_FDIM, o1 + (d + 1) * HG_FDIM)
        cols = slice(d * HG_FDIM, (d + 1) * HG_FDIM)
        lb = lb_ref[:, cols]
        sg = jax.nn.sigmoid(z)
        f = lb + (1.0 - lb) * sg
        lf = jnp.log2(f)
        lfh = lf.astype(BF16)
        lfh_ref[:, cols] = lfh
        lfl_ref[:, cols] = (lf - lfh.astype(F32)).astype(BF16)
        kk_ref[:, cols] = ((1.0 - lb) * (1.0 - sg)).astype(BF16)
    nrow = u_scr.shape[1] // S5_L
    seg = lax.broadcasted_iota(jnp.int32, (nrow, LANES), 1) // S5_GROUP
    pu = proj(o4, IN_WIDTH)
    for o in range(S5_OCTETS):
        u_scr[o] = pu[:, o * LANES:(o + 1) * LANES]
        src = [u_scr[o, pl.ds(tau, nrow, stride=S5_L), :] for tau in range(S5_L)]
        for half in range(S5_L // S5_GPO):
            for i in range(S5_GPO):
                u_ref[o * S5_GPO + i, :, half * LANES:(half + 1) * LANES] = _regroup(
                    src[half * S5_GPO:(half + 1) * S5_GPO], i, seg).astype(BF16)
    q_ref[...] = proj(0, o1).astype(BF16)
    v_ref[...] = proj(o2, o3).astype(BF16)
    g_ref[...] = proj(o3, o4).astype(BF16)


def _inproj(x, g1, w_in, lb, tm):
    t = x.shape[0]
    row = lambda w: pl.BlockSpec((tm, w), lambda i: (i, 0))
    outs = [
        jax.ShapeDtypeStruct((t, HG_FDIM), BF16),
        jax.ShapeDtypeStruct((t, 2 * HG_FDIM), BF16),
        jax.ShapeDtypeStruct((t, 2 * HG_FDIM), BF16),
        jax.ShapeDtypeStruct((t, 2 * HG_FDIM), BF16),
        jax.ShapeDtypeStruct((t, HG_WIDTH), BF16),
        jax.ShapeDtypeStruct((t, HG_WIDTH), BF16),
        jax.ShapeDtypeStruct((S5_GROUPS, t // S5_L, S5_CW), BF16),
    ]
    return pl.pallas_call(
        _inproj_kernel,
        grid=(t // tm,),
        in_specs=[row(D_MODEL), _const_spec((1, D_MODEL)), _const_spec((D_MODEL, IN_WIDTH)),
                  _const_spec((1, 2 * HG_FDIM))],
        out_specs=[row(HG_FDIM), row(2 * HG_FDIM), row(2 * HG_FDIM), row(2 * HG_FDIM),
                   row(HG_WIDTH), row(HG_WIDTH),
                   pl.BlockSpec((S5_GROUPS, tm // S5_L, S5_CW), lambda i: (0, i, 0))],
        out_shape=outs,
        scratch_shapes=[pltpu.VMEM((S5_OCTETS, tm, LANES), F32)],
        compiler_params=_cparams(("parallel",)),
        name="inproj",
    )(x, g1, w_in, lb)


def _hgrn_constants():
    c = HG_C
    idx = np.arange(c)
    blocks = [(idx[None, :] <= idx[:, None])]
    masks = []
    for m in HG_LEVELS:
        rho = (idx // (2 * m)) * (2 * m) + m
        late = idx >= rho
        x = idx[None, :]
        r = idx[:, None]
        d = np.where(late[:, None], (x >= rho[:, None]) & (x <= r), (x > r) & (x < rho[:, None]))
        if 1 < m < HG_SUB:
            blocks.append(d)
        same = (idx[:, None] // (2 * m)) == (idx[None, :] // (2 * m))
        masks.append(same & late[:, None] & (~late)[None, :])
    d_fwd = np.concatenate(blocks, axis=0).astype(np.float32)
    d_bwd = np.concatenate([b[::-1, ::-1] for b in blocks], axis=0).astype(np.float32)
    m_fwd = np.stack(masks).astype(np.float32)
    m_bwd = m_fwd[:, ::-1, ::-1]
    dall = np.stack([d_fwd, d_bwd])
    dall = np.concatenate([dall, dall], axis=-1)
    return dall, np.stack([m_fwd, m_bwd])


def _hgrn_group(q, k, lfh, lfl, v, st_ref, dall, mask_ref, reverse):
    c = HG_C
    nc = q.shape[0] // c
    nt = (((1,), (1,)), ((), ()))
    tn = (((0,), (0,)), ((), ()))
    sl = lambda a, j: a[j * c:(j + 1) * c]
    lf2 = jnp.concatenate(
        [jnp.concatenate([sl(lfh, j), sl(lfl, j)], axis=0) for j in range(nc)], axis=1)
    e_all = jnp.dot(dall, lf2, preferred_element_type=F32)
    row = lax.broadcasted_iota(jnp.int32, (c, HG_DK), 0)
    if reverse:
        row = (c - 1) - row
    sub = HG_SUB
    ones = jnp.ones((HG_DK, c), BF16)
    eye = lax.broadcasted_iota(jnp.int32, (c, c), 0) == lax.broadcasted_iota(jnp.int32, (c, c), 1)
    dg_all = jnp.dot(q * k, ones, preferred_element_type=F32)
    parts = []
    for j in range(nc):
        ej = e_all[:, j * HG_DK:(j + 1) * HG_DK]
        bc = ej[0:c]
        qf = sl(q, j).astype(F32)
        kf = sl(k, j).astype(F32)
        vj = sl(v, j)
        att = [jnp.zeros((sub, c), F32) for _ in range(c // sub)]
        for li, m in enumerate(HG_LEVELS):
            msk = mask_ref[li]
            if m >= sub:
                pieces = []
                for pb in range(c // (2 * m)):
                    lo, mid, hi = pb * 2 * m, pb * 2 * m + m, (pb + 1) * 2 * m
                    ref = bc[mid:mid + 1] if reverse else bc[mid - 1:mid]
                    first, second = bc[lo:mid], bc[mid:hi]
                    pieces += ([first - ref, ref - second] if reverse else [ref - first, second - ref])
                g_l = jnp.exp2(jnp.concatenate(pieces, axis=0))
                late = [b for b in range(c // m) if ((c // m - 1 - b) if reverse else b) % 2 == 1]
                qk = jnp.concatenate(
                    [(qf if b in late else kf)[b * m:(b + 1) * m] for b in range(c // m)], axis=0)
                xl = (qk * g_l).astype(BF16)
                xq = jnp.concatenate([xl[b * m:(b + 1) * m] for b in late], axis=0)
                a_l = lax.dot_general(xq, xl, nt, preferred_element_type=F32)
                for n, b in enumerate(late):
                    for r in range(m // sub):
                        t = b * (m // sub) + r
                        lo = n * m + r * sub
                        att[t] = att[t] + a_l[lo:lo + sub] * msk[t * sub:(t + 1) * sub]
            else:
                late_row = (row & (2 * m - 1)) >= m
                if m == 1:
                    g_l = jnp.exp2(jnp.where(late_row, sl(lfh, j).astype(F32) + sl(lfl, j).astype(F32), 0.0))
                else:
                    nf = 1 + li - sum(mm >= sub for mm in HG_LEVELS)
                    g_l = jnp.exp2(ej[nf * c:(nf + 1) * c])
                xl = (jnp.where(late_row, qf, kf) * g_l).astype(BF16)
                a_l = lax.dot_general(xl, xl, nt, preferred_element_type=F32)
                for t in range(c // sub):
                    att[t] = att[t] + a_l[t * sub:(t + 1) * sub] * msk[t * sub:(t + 1) * sub]
        att = jnp.concatenate(att, axis=0) + jnp.where(eye, sl(dg_all, j), 0.0)
        intra = jnp.dot(att.astype(BF16), vj, preferred_element_type=F32)
        g_cum = jnp.exp2(bc)
        qt = (qf * g_cum).astype(BF16)
        tot = bc[0:1] if reverse else bc[c - 1:c]
        kh = (kf * jnp.exp2(tot - bc)).astype(BF16)
        upd = lax.dot_general(kh, vj, tn, preferred_element_type=F32)
        g_tot = g_cum[0:1] if reverse else g_cum[c - 1:c]
        parts.append((intra, qt, g_tot, upd))
    st = st_ref[...]
    outs = [None] * nc
    for j in (range(nc - 1, -1, -1) if reverse else range(nc)):
        intra, qt, g_tot, upd = parts[j]
        outs[j] = intra + jnp.dot(qt, st.astype(BF16), preferred_element_type=F32)
        st = st * jnp.transpose(jnp.broadcast_to(g_tot, (HG_DV, HG_DK))) + upd
    st_ref[...] = st
    return jnp.concatenate(outs, axis=0)


def _hgrn_kernel(qf_ref, kf_ref, hf_ref, lf_ref, vf_ref, qb_ref, kb_ref, hb_ref, lb_ref, vb_ref,
                 dall_ref, mask_ref, of_ref, ob_ref, st_ref):
    @pl.when(pl.program_id(2) == 0)
    def _():
        st_ref[...] = jnp.zeros_like(st_ref)

    of_ref[...] = _hgrn_group(qf_ref[...], kf_ref[...], hf_ref[...], lf_ref[...], vf_ref[...],
                              st_ref.at[0], dall_ref[0], mask_ref.at[0], False)
    ob_ref[...] = _hgrn_group(qb_ref[...], kb_ref[...], hb_ref[...], lb_ref[...], vb_ref[...],
                              st_ref.at[1], dall_ref[1], mask_ref.at[1], True)


def _hgrn(q, lfh, lfl, kk, v, n_seq, seq, ts):
    t = q.shape[0]
    nt = seq // ts
    assert seq % ts == 0 and ts % HG_C == 0
    dall, masks = _hgrn_constants()
    dall = jnp.asarray(dall, BF16)
    masks = jnp.asarray(masks, F32)
    fwd = lambda off: pl.BlockSpec((ts, HG_DK), lambda b, h, j: (b * nt + j, h + off))
    bwd = lambda off: pl.BlockSpec((ts, HG_DK), lambda b, h, j: (b * nt + (nt - 1 - j), h + off))
    out = jax.ShapeDtypeStruct((t, HG_WIDTH), F32)
    return pl.pallas_call(
        _hgrn_kernel,
        grid=(n_seq, HG_HEADS, nt),
        in_specs=[fwd(0), fwd(0), fwd(0), fwd(0), fwd(0),
                  bwd(0), bwd(HG_HEADS), bwd(HG_HEADS), bwd(HG_HEADS), bwd(0),
                  _const_spec(dall.shape), _const_spec(masks.shape)],
        out_specs=[fwd(0), bwd(0)],
        out_shape=[out, out],
        scratch_shapes=[pltpu.VMEM((2, HG_DK, HG_DV), F32)],
        compiler_params=_cparams(("parallel", "parallel", "arbitrary")),
        name="hgrn2",
    )(q, kk, lfh, lfl, v, q, kk, lfh, lfl, v, dall, masks)


def _s5_matrices(lam_re, lam_im, log_dt, b_re, b_im, c_re, c_im, d_skip):
    hp = lax.Precision.HIGHEST
    ell, g, c, p = S5_L, S5_GROUPS, S5_GROUP, S5_STATE
    no, gpo = S5_OCTETS, S5_GPO
    dt = jnp.exp(log_dt)[..., None]

    def cpow(j):
        j = jnp.asarray(j, F32).reshape((-1, 1, 1, 1))
        mag = jnp.exp(lam_re * dt * j)
        ang = lam_im * dt * j
        return mag * jnp.cos(ang), mag * jnp.sin(ang)

    pr, pi = cpow(np.arange(ell + 1))
    abar_re, abar_im = pr[1], pi[1]
    den = lam_re * lam_re + lam_im * lam_im
    nr = abar_re - 1.0
    ni = abar_im
    cr = ((nr * lam_re + ni * lam_im) / den)[..., None]
    ci = ((ni * lam_re - nr * lam_im) / den)[..., None]
    bb_re = jnp.swapaxes(cr * b_re - ci * b_im, -1, -2)
    bb_im = jnp.swapaxes(cr * b_im + ci * b_re, -1, -2)

    prb, pib = pr[:, :, :, None, :], pi[:, :, :, None, :]
    wr = prb * bb_re - pib * bb_im
    wi = prb * bb_im + pib * bb_re
    kj = jnp.einsum('dgcp,jdgep->jdgce', jnp.concatenate([c_re, -c_im], axis=-1),
                    jnp.concatenate([wr[:ell], wi[:ell]], axis=-1), precision=hp)
    tau = np.arange(ell)

    skip = jnp.eye(c, dtype=F32) * d_skip.reshape(g, c, 1)
    klag = jnp.concatenate([kj[ell - 1:0:-1, 1], (kj[0, 0] + kj[0, 1] + skip)[None], kj[1:ell, 0]], axis=0)
    lag = tau[None, :] - tau[:, None] + (ell - 1)
    tz = klag[lag].transpose(2, 0, 4, 1, 3).reshape(g, S5_CW, S5_CW)

    mf_re, mf_im = wr[ell - 1 - tau, 0], wi[ell - 1 - tau, 0]
    mb_re, mb_im = wr[tau, 1], wi[tau, 1]
    mv = jnp.concatenate([mf_re, mb_re, mf_im, mb_im], axis=-1)
    mv = jnp.swapaxes(mv, 0, 1).reshape(g, S5_CW, S5_GW)

    def cw(powers, d):
        p_re = pr[powers, d][:, :, None, :]
        p_im = pi[powers, d][:, :, None, :]
        return c_re[d] * p_re - c_im[d] * p_im, c_re[d] * p_im + c_im[d] * p_re

    nf_re, nf_im = cw(tau + 1, 0)
    nb_re, nb_im = cw(ell - tau, 1)
    nm = jnp.concatenate([nf_re, nb_re, -nf_im, -nb_im], axis=-1)
    nm = jnp.swapaxes(jnp.swapaxes(nm, 0, 1).reshape(g, S5_CW, S5_GW), 1, 2)

    a_re = jnp.concatenate([pr[ell, 0], pr[ell, 1]], axis=-1).reshape(no, gpo * 2 * p)
    a_im = jnp.concatenate([pi[ell, 0], pi[ell, 1]], axis=-1).reshape(no, gpo * 2 * p)
    a16 = jnp.stack([a_re, a_im], axis=1)
    return tz.astype(BF16), mv.astype(BF16), nm.astype(BF16), a16.astype(F32)


def _s5_state_kernel(u_ref, mv_ref, a_ref, x_ref, v_scr, *, nseq, nchunk, segmented):
    hw = S5_SW // 2
    ntile = S5_SW // LANES
    gpo = S5_GPO

    def ld(r):
        return jnp.concatenate([v_scr[k, r, :] for k in range(ntile)], axis=1)

    def st(r, val):
        for k in range(ntile):
            v_scr[k, r, :] = val[:, k * LANES:(k + 1) * LANES]

    for i in range(gpo):
        inc = jnp.dot(u_ref[i], mv_ref[i], preferred_element_type=F32)
        for s in range(nseq):
            dst = pl.ds(s, nchunk, stride=nseq)
            v_scr[i, dst, :] = inc[s * nchunk:(s + 1) * nchunk, :LANES]
            v_scr[gpo + i, dst, :] = inc[s * nchunk:(s + 1) * nchunk, LANES:]
    a = a_ref[0]
    ar = jnp.broadcast_to(a[0:1], (nseq, hw))
    ai = jnp.broadcast_to(a[1:2], (nseq, hw))
    lane = lax.broadcasted_iota(jnp.int32, (nseq, S5_SW), 1)
    is_fwd2 = (lane & (2 * S5_STATE - 1)) < S5_STATE
    is_fwd = is_fwd2[:, :hw]

    def rows(i):
        rf = pl.ds(pl.multiple_of(i * nseq, nseq), nseq)
        rb = pl.ds(pl.multiple_of((nchunk - 1 - i) * nseq, nseq), nseq)
        return rf, rb

    def body(i, carry):
        xr, xi = carry
        rf, rb = rows(i)
        vf = ld(rf)
        vb = ld(rb)
        x = jnp.concatenate([xr, xi], axis=1)
        st(rf, jnp.where(is_fwd2, x, vf))
        st(rb, jnp.where(is_fwd2, vb, x))
        vr = jnp.where(is_fwd, vf[:, :hw], vb[:, :hw])
        vi = jnp.where(is_fwd, vf[:, hw:], vb[:, hw:])
        return ar * xr - ai * xi + vr, ar * xi + ai * xr + vi

    zero = jnp.zeros((nseq, hw), F32)
    er, ei = lax.fori_loop(0, nchunk, body, (zero, zero))

    if segmented:
        sr, si = a[0:1], a[1:2]
        for _ in range(nchunk.bit_length() - 1):
            sr, si = sr * sr - si * si, 2.0 * sr * si
        z1 = jnp.zeros((1, hw), F32)
        cf = [(z1, z1)]
        for s in range(nseq - 1):
            pr_, pi_ = cf[-1]
            cf.append((er[s:s + 1] + sr * pr_ - si * pi_, ei[s:s + 1] + sr * pi_ + si * pr_))
        cb = [(z1, z1)]
        for s in range(nseq - 1, 0, -1):
            pr_, pi_ = cb[0]
            cb.insert(0, (er[s:s + 1] + sr * pr_ - si * pi_, ei[s:s + 1] + sr * pi_ + si * pr_))
        cr = jnp.where(is_fwd, jnp.concatenate([t[0] for t in cf], axis=0),
                       jnp.concatenate([t[0] for t in cb], axis=0))
        ci = jnp.where(is_fwd, jnp.concatenate([t[1] for t in cf], axis=0),
                       jnp.concatenate([t[1] for t in cb], axis=0))

        def fix(i, carry):
            cr, ci = carry
            rf, rb = rows(i)
            x = jnp.concatenate([cr, ci], axis=1)
            st(rf, ld(rf) + jnp.where(is_fwd2, x, 0.0))
            st(rb, ld(rb) + jnp.where(is_fwd2, 0.0, x))
            return ar * cr - ai * ci, ar * ci + ai * cr

        lax.fori_loop(0, nchunk, fix, (cr, ci))

    for i in range(gpo):
        for s in range(nseq):
            src = pl.ds(s, nchunk, stride=nseq)
            x_ref[i, s * nchunk:(s + 1) * nchunk, :] = jnp.concatenate(
                [v_scr[i, src, :], v_scr[gpo + i, src, :]], axis=1).astype(BF16)


def _s5_out_kernel(u_ref, x_ref, tz_ref, nm_ref, y_ref):
    for i in range(S5_GPO):
        y = jnp.dot(u_ref[i], tz_ref[i], preferred_element_type=F32)
        y = y + jnp.dot(x_ref[i], nm_ref[i], preferred_element_type=F32)
        y_ref[i] = y.astype(BF16)


def _s5(ug, n_seq, seq, mats, tm_out):
    tz, mv, nm, a16 = mats
    nrow = ug.shape[1]
    nseq = S5_ROWS
    segmented = n_seq == 1
    if segmented:
        n_seq, seq = nseq, seq // nseq
    assert n_seq % nseq == 0 and seq % S5_L == 0
    nchunk = seq // S5_L
    assert nchunk & (nchunk - 1) == 0 and nchunk % 16 == 0
    m = nchunk * nseq
    blk = lambda r: pl.BlockSpec((S5_GPO, r, S5_CW), lambda o, i: (o, i, 0))
    mat = lambda r, c: pl.BlockSpec((S5_GPO, r, c), lambda o, i: (o, 0, 0))
    xin = pl.pallas_call(
        functools.partial(_s5_state_kernel, nseq=nseq, nchunk=nchunk, segmented=segmented),
        grid=(S5_OCTETS, nrow // m),
        in_specs=[blk(m), mat(S5_CW, S5_GW),
                  pl.BlockSpec((1, 2, S5_SW // 2), lambda o, i: (o, 0, 0))],
        out_specs=blk(m),
        out_shape=jax.ShapeDtypeStruct((S5_GROUPS, nrow, S5_GW), BF16),
        scratch_shapes=[pltpu.VMEM((S5_SW // LANES, m, LANES), F32)],
        compiler_params=_cparams(("parallel", "arbitrary")),
        name="s5_state",
    )(ug, mv, a16)
    return pl.pallas_call(
        _s5_out_kernel,
        grid=(S5_OCTETS, nrow // tm_out),
        in_specs=[blk(tm_out), blk(tm_out), mat(S5_CW, S5_CW), mat(S5_GW, S5_CW)],
        out_specs=blk(tm_out),
        out_shape=jax.ShapeDtypeStruct((S5_GROUPS, nrow, S5_CW), BF16),
        compiler_params=_cparams(("parallel", "arbitrary")),
        name="s5_out",
    )(ug, xin, tz, nm)


def _rms(x, g):
    return x * lax.rsqrt(jnp.mean(x * x, axis=-1, keepdims=True) + EPS) * g


def _out_kernel(x_ref, of_ref, ob_ref, g_ref, y2_ref, onorm_ref, gluw_ref, glub_ref,
                merge_ref, wout_ref, n2_ref, w1_ref, w2_ref, nf_ref, out_ref, y_scr):
    o = of_ref[...] + ob_ref[...]
    heads = []
    for h in range(HG_HEADS):
        oh = o[:, h * HG_DV:(h + 1) * HG_DV]
        heads.append(oh * lax.rsqrt(jnp.mean(oh * oh, axis=-1, keepdims=True) + EPS))
    g = g_ref[...].astype(F32)
    ya = jnp.concatenate(heads, axis=1) * onorm_ref[...] * (g * jax.nn.sigmoid(g))

    nrow = y_scr.shape[1] // S5_L
    seg = lax.broadcasted_iota(jnp.int32, (nrow, LANES), 1) // S5_GROUP
    for oc in range(S5_OCTETS):
        for half in range(S5_L // S5_GPO):
            src = [y2_ref[oc * S5_GPO + i, :, half * LANES:(half + 1) * LANES].astype(F32)
                   for i in range(S5_GPO)]
            for t8 in range(S5_GPO):
                y_scr[oc, pl.ds(half * S5_GPO + t8, nrow, stride=S5_L), :] = _regroup(src, t8, seg)
    z = jax.nn.gelu(jnp.concatenate([y_scr[oc] for oc in range(S5_OCTETS)], axis=1))
    gate = jnp.dot(z.astype(BF16), gluw_ref[...], preferred_element_type=F32) + glub_ref[...]
    z = z * jax.nn.sigmoid(gate)
    yb = _rms(z, merge_ref[...])

    ycat = jnp.concatenate([ya, yb], axis=1).astype(BF16)
    x1 = x_ref[...] + jnp.dot(ycat, wout_ref[...], preferred_element_type=F32)
    h2 = _rms(x1, n2_ref[...]).astype(BF16)
    acc = x1
    step = D_FF // 4
    for c in range(4):
        hid = jnp.dot(h2, w1_ref[:, c * step:(c + 1) * step], preferred_element_type=F32)
        hid = jnp.square(jnp.maximum(hid, 0.0)).astype(BF16)
        acc = acc + jnp.dot(hid, w2_ref[c * step:(c + 1) * step, :], preferred_element_type=F32)
    out_ref[...] = _rms(acc, nf_ref[...])


def _outstage(x, o_f, o_b, g, y2, onorm, gluw, glub, merge, wout, n2, w1, w2, nf, tm):
    t = x.shape[0]
    row = lambda w: pl.BlockSpec((tm, w), lambda i: (i, 0))
    vec = lambda w: _const_spec((1, w))
    return pl.pallas_call(
        _out_kernel,
        grid=(t // tm,),
        in_specs=[row(D_MODEL), row(HG_WIDTH), row(HG_WIDTH), row(HG_WIDTH),
                  pl.BlockSpec((S5_GROUPS, tm // S5_L, S5_CW), lambda i: (0, i, 0)),
                  vec(HG_WIDTH), _const_spec((S5_WIDTH, S5_WIDTH)), vec(S5_WIDTH),
                  vec(S5_WIDTH), _const_spec((D_MODEL, D_MODEL)), vec(D_MODEL),
                  _const_spec((D_MODEL, D_FF)), _const_spec((D_FF, D_MODEL)), vec(D_MODEL)],
        out_specs=row(D_MODEL),
        out_shape=jax.ShapeDtypeStruct((t, D_MODEL), F32),
        scratch_shapes=[pltpu.VMEM((S5_OCTETS, tm, LANES), F32)],
        compiler_params=_cparams(("parallel",)),
        name="outstage",
    )(x, o_f, o_b, g, y2, onorm, gluw, glub, merge, wout, n2, w1, w2, nf)


def _trunk(x3, p):
    b, s, dm = x3.shape
    x = x3.reshape(b * s, dm)
    q, lfh, lfl, kk, v, g, u2 = _inproj(x, p["g1"], p["w_in"], p["lb"], ROW_TILE_IN)
    o_f, o_b = _hgrn(q, lfh, lfl, kk, v, b, s, min(HG_TILE, s))
    y2 = _s5(u2, b, s, p["s5"], min(S5_TILE_OUT, b * s // S5_L))
    out = _outstage(x, o_f, o_b, g, y2, p["onorm"], p["gluw"], p["glub"], p["merge"],
                    p["wout"], p["n2"], p["w1"], p["w2"], p["nf"], ROW_TILE_OUT)
    return out.reshape(b, s, dm)


def kernel(x_prompt, x_sample, norm1_g, w_in, hgrn_lb, hgrn_onorm_g, s5_lambda_re, s5_lambda_im,
           s5_log_dt, s5_b_re, s5_b_im, s5_c_re, s5_c_im, s5_d, s5_glu_w, s5_glu_b, s5_merge_g,
           w_out, norm2_g, w_ff1, w_ff2, norm_f_g):
    l = 0
    lbs = jnp.cumsum(jax.nn.softmax(hgrn_lb.astype(F32), axis=1), axis=1)[:, l]
    row = lambda a: a.astype(F32).reshape(1, -1)
    f = lambda a: a[l].astype(F32)
    p = {
        "g1": row(norm1_g[l]),
        "w_in": w_in[l].astype(BF16),
        "lb": lbs.reshape(1, 2 * HG_FDIM),
        "s5": _s5_matrices(f(s5_lambda_re), f(s5_lambda_im), f(s5_log_dt), f(s5_b_re), f(s5_b_im),
                           f(s5_c_re), f(s5_c_im), f(s5_d)),
        "onorm": row(hgrn_onorm_g[l]),
        "gluw": s5_glu_w[l].astype(BF16),
        "glub": row(s5_glu_b[l]),
        "merge": row(s5_merge_g[l]),
        "wout": w_out[l].astype(BF16),
        "n2": row(norm2_g[l]),
        "w1": w_ff1[l].astype(BF16),
        "w2": w_ff2[l].astype(BF16),
        "nf": row(norm_f_g),
    }
    return (_trunk(x_prompt, p), _trunk(x_sample, p))
```

```python
import functools

import numpy as np
import jax
import jax.numpy as jnp
from jax import lax
from jax.experimental import pallas as pl
from jax.experimental.pallas import tpu as pltpu

F32 = jnp.float32
BF16 = jnp.bfloat16

D_MODEL = 1024
HG_HEADS = 4
HG_DK = 128
HG_DV = 128
HG_FDIM = HG_HEADS * HG_DK
HG_WIDTH = HG_HEADS * HG_DV
S5_WIDTH = 512
S5_GROUP = 16
S5_GROUPS = S5_WIDTH // S5_GROUP
S5_STATE = 64
D_FF = 4 * D_MODEL
IN_WIDTH = 3 * HG_FDIM + 2 * HG_WIDTH + S5_WIDTH
EPS = 1e-6

LANES = 128
VMEM_LIMIT_BYTES = 56 * 1024 * 1024

ROW_TILE_IN = 512
ROW_TILE_OUT = 512
HG_TILE = 2048

HG_C = 64
HG_LEVELS = (32, 16, 8, 4, 2, 1)
HG_SUB = 8
HG_NBLK = 1 + sum(1 < m < HG_SUB for m in HG_LEVELS)

S5_L = 16
S5_OCTETS = S5_WIDTH // LANES
S5_GPO = LANES // S5_GROUP
S5_CW = S5_L * S5_GROUP
S5_GW = 4 * S5_STATE
S5_SW = S5_GW * S5_GPO
S5_ROWS = 8


def _cparams(sem):
    return pltpu.CompilerParams(dimension_semantics=sem, vmem_limit_bytes=VMEM_LIMIT_BYTES)


def _const_spec(shape):
    n = len(shape)
    return pl.BlockSpec(shape, lambda *_: (0,) * n, pipeline_mode=pl.Buffered(1))


def _regroup(srcs, sel, seg):
    out = None
    for k, s in enumerate(srcs):
        shift = ((k - sel) % S5_GPO) * S5_GROUP
        r = s if shift == 0 else pltpu.roll(s, shift, axis=1)
        out = r if out is None else jnp.where(seg == k, r, out)
    return out


def _inproj_kernel(x_ref, g1_ref, w_ref, lb_ref, q_ref, lfh_ref, lfl_ref, kk_ref, v_ref, g_ref, u_ref,
                   u_scr):
    x = x_ref[...]
    ms = jnp.mean(x * x, axis=-1, keepdims=True)
    h = (x * lax.rsqrt(ms + EPS) * g1_ref[...]).astype(BF16)
    o1 = HG_FDIM
    o2 = 3 * HG_FDIM
    o3 = o2 + HG_WIDTH
    o4 = o3 + HG_WIDTH
    proj = lambda lo, hi: jnp.dot(h, w_ref[:, lo:hi], preferred_element_type=F32)
    for d in range(2):
        z = proj(o1 + d * HG_FDIM, o1 + (d + 1) * HG_FDIM)
        cols = slice(d * HG_FDIM, (d + 1) * HG_FDIM)
        lb = lb_ref[:, cols]
        sg = jax.nn.sigmoid(z)
        f = lb + (1.0 - lb) * sg
        lf = jnp.log2(f)
        lfh = lf.astype(BF16)
        lfh_ref[:, cols] = lfh
        lfl_ref[:, cols] = (lf - lfh.astype(F32)).astype(BF16)
        kk_ref[:, cols] = ((1.0 - lb) * (1.0 - sg)).astype(BF16)
    nrow = u_scr.shape[1] // S5_L
    seg = lax.broadcasted_iota(jnp.int32, (nrow, LANES), 1) // S5_GROUP
    pu = proj(o4, IN_WIDTH)
    for o in range(S5_OCTETS):
        u_scr[o] = pu[:, o * LANES:(o + 1) * LANES]
        src = [u_scr[o, pl.ds(tau, nrow, stride=S5_L), :] for tau in range(S5_L)]
        for half in range(S5_L // S5_GPO):
            for i in range(S5_GPO):
                u_ref[o * S5_GPO + i, :, half * LANES:(half + 1) * LANES] = _regroup(
                    src[half * S5_GPO:(half + 1) * S5_GPO], i, seg).astype(BF16)
    q_ref[...] = proj(0, o1).astype(BF16)
    v_ref[...] = proj(o2, o3).astype(BF16)
    g_ref[...] = proj(o3, o4).astype(BF16)


def _inproj(x, g1, w_in, lb, tm):
    t = x.shape[0]
    row = lambda w: pl.BlockSpec((tm, w), lambda i: (i, 0))
    outs = [
        jax.ShapeDtypeStruct((t, HG_FDIM), BF16),
        jax.ShapeDtypeStruct((t, 2 * HG_FDIM), BF16),
        jax.ShapeDtypeStruct((t, 2 * HG_FDIM), BF16),
        jax.ShapeDtypeStruct((t, 2 * HG_FDIM), BF16),
        jax.ShapeDtypeStruct((t, HG_WIDTH), BF16),
        jax.ShapeDtypeStruct((t, HG_WIDTH), BF16),
        jax.ShapeDtypeStruct((S5_GROUPS, t // S5_L, S5_CW), BF16),
    ]
    return pl.pallas_call(
        _inproj_kernel,
        grid=(t // tm,),
        in_specs=[row(D_MODEL), _const_spec((1, D_MODEL)), _const_spec((D_MODEL, IN_WIDTH)),
                  _const_spec((1, 2 * HG_FDIM))],
        out_specs=[row(HG_FDIM), row(2 * HG_FDIM), row(2 * HG_FDIM), row(2 * HG_FDIM),
                   row(HG_WIDTH), row(HG_WIDTH),
                   pl.BlockSpec((S5_GROUPS, tm // S5_L, S5_CW), lambda i: (0, i, 0))],
        out_shape=outs,
        scratch_shapes=[pltpu.VMEM((S5_OCTETS, tm, LANES), F32)],
        compiler_params=_cparams(("parallel",)),
        name="inproj",
    )(x, g1, w_in, lb)


def _hgrn_constants():
    c = HG_C
    idx = np.arange(c)
    blocks = [(idx[None, :] <= idx[:, None])]
    masks = []
    for m in HG_LEVELS:
        rho = (idx // (2 * m)) * (2 * m) + m
        late = idx >= rho
        x = idx[None, :]
        r = idx[:, None]
        d = np.where(late[:, None], (x >= rho[:, None]) & (x <= r), (x > r) & (x < rho[:, None]))
        if 1 < m < HG_SUB:
            blocks.append(d)
        same = (idx[:, None] // (2 * m)) == (idx[None, :] // (2 * m))
        masks.append(same & late[:, None] & (~late)[None, :])
    d_fwd = np.concatenate(blocks, axis=0).astype(np.float32)
    d_bwd = np.concatenate([b[::-1, ::-1] for b in blocks], axis=0).astype(np.float32)
    m_fwd = np.stack(masks).astype(np.float32)
    m_bwd = m_fwd[:, ::-1, ::-1]
    dall = np.stack([d_fwd, d_bwd])
    dall = np.concatenate([dall, dall], axis=-1)
    return dall, np.stack([m_fwd, m_bwd])


def _hgrn_group(q, k, lfh, lfl, v, st_ref, dall, mask_ref, reverse):
    c = HG_C
    nc = q.shape[0] // c
    nt = (((1,), (1,)), ((), ()))
    tn = (((0,), (0,)), ((), ()))
    sl = lambda a, j: a[j * c:(j + 1) * c]
    lf2 = jnp.concatenate(
        [jnp.concatenate([sl(lfh, j), sl(lfl, j)], axis=0) for j in range(nc)], axis=1)
    e_all = jnp.dot(dall, lf2, preferred_element_type=F32)
    row = lax.broadcasted_iota(jnp.int32, (c, HG_DK), 0)
    if reverse:
        row = (c - 1) - row
    sub = HG_SUB
    ones = jnp.ones((HG_DK, c), BF16)
    eye = lax.broadcasted_iota(jnp.int32, (c, c), 0) == lax.broadcasted_iota(jnp.int32, (c, c), 1)
    dg_all = jnp.dot(q * k, ones, preferred_element_type=F32)
    parts = []
    for j in range(nc):
        ej = e_all[:, j * HG_DK:(j + 1) * HG_DK]
        bc = ej[0:c]
        qf = sl(q, j).astype(F32)
        kf = sl(k, j).astype(F32)
        vj = sl(v, j)
        att = [jnp.zeros((sub, c), F32) for _ in range(c // sub)]
        for li, m in enumerate(HG_LEVELS):
            msk = mask_ref[li]
            if m >= sub:
                pieces = []
                for pb in range(c // (2 * m)):
                    lo, mid, hi = pb * 2 * m, pb * 2 * m + m, (pb + 1) * 2 * m
                    ref = bc[mid:mid + 1] if reverse else bc[mid - 1:mid]
                    first, second = bc[lo:mid], bc[mid:hi]
                    pieces += ([first - ref, ref - second] if reverse else [ref - first, second - ref])
                g_l = jnp.exp2(jnp.concatenate(pieces, axis=0))
                late = [b for b in range(c // m) if ((c // m - 1 - b) if reverse else b) % 2 == 1]
                qk = jnp.concatenate(
                    [(qf if b in late else kf)[b * m:(b + 1) * m] for b in range(c // m)], axis=0)
                xl = (qk * g_l).astype(BF16)
                xq = jnp.concatenate([xl[b * m:(b + 1) * m] for b in late], axis=0)
                a_l = lax.dot_general(xq, xl, nt, preferred_element_type=F32)
                for n, b in enumerate(late):
                    for r in range(m // sub):
                        t = b * (m // sub) + r
                        lo = n * m + r * sub
                        att[t] = att[t] + a_l[lo:lo + sub] * msk[t * sub:(t + 1) * sub]
            else:
                late_row = (row & (2 * m - 1)) >= m
                if m == 1:
                    g_l = jnp.exp2(jnp.where(late_row, sl(lfh, j).astype(F32) + sl(lfl, j).astype(F32), 0.0))
                else:
                    nf = 1 + li - sum(mm >= sub for mm in HG_LEVELS)
                    g_l = jnp.exp2(ej[nf * c:(nf + 1) * c])
                xl = (jnp.where(late_row, qf, kf) * g_l).astype(BF16)
                a_l = lax.dot_general(xl, xl, nt, preferred_element_type=F32)
                for t in range(c // sub):
                    att[t] = att[t] + a_l[t * sub:(t + 1) * sub] * msk[t * sub:(t + 1) * sub]
        att = jnp.concatenate(att, axis=0) + jnp.where(eye, sl(dg_all, j), 0.0)
        intra = jnp.dot(att.astype(BF16), vj, preferred_element_type=F32)
        g_cum = jnp.exp2(bc)
        qt = (qf * g_cum).astype(BF16)
        tot = bc[0:1] if reverse else bc[c - 1:c]
        kh = (kf * jnp.exp2(tot - bc)).astype(BF16)
        upd = lax.dot_general(kh, vj, tn, preferred_element_type=F32)
        g_tot = g_cum[0:1] if reverse else g_cum[c - 1:c]
        parts.append((intra, qt, g_tot, upd))
    st = st_ref[...]
    outs = [None] * nc
    for j in (range(nc - 1, -1, -1) if reverse else range(nc)):
        intra, qt, g_tot, upd = parts[j]
        outs[j] = intra + jnp.dot(qt, st.astype(BF16), preferred_element_type=F32)
        st = st * jnp.transpose(jnp.broadcast_to(g_tot, (HG_DV, HG_DK))) + upd
    st_ref[...] = st
    return jnp.concatenate(outs, axis=0)


def _hgrn_kernel(qf_ref, kf_ref, hf_ref, lf_ref, vf_ref, qb_ref, kb_ref, hb_ref, lb_ref, vb_ref,
                 dall_ref, mask_ref, of_ref, ob_ref, st_ref):
    @pl.when(pl.program_id(2) == 0)
    def _():
        st_ref[...] = jnp.zeros_like(st_ref)

    of_ref[...] = _hgrn_group(qf_ref[...], kf_ref[...], hf_ref[...], lf_ref[...], vf_ref[...],
                              st_ref.at[0], dall_ref[0], mask_ref.at[0], False)
    ob_ref[...] = _hgrn_group(qb_ref[...], kb_ref[...], hb_ref[...], lb_ref[...], vb_ref[...],
                              st_ref.at[1], dall_ref[1], mask_ref.at[1], True)


def _hgrn(q, lfh, lfl, kk, v, n_seq, seq, ts):
    t = q.shape[0]
    nt = seq // ts
    assert seq % ts == 0 and ts % HG_C == 0
    dall, masks = _hgrn_constants()
    dall = jnp.asarray(dall, BF16)
    masks = jnp.asarray(masks, F32)
    fwd = lambda off: pl.BlockSpec((ts, HG_DK), lambda b, h, j: (b * nt + j, h + off))
    bwd = lambda off: pl.BlockSpec((ts, HG_DK), lambda b, h, j: (b * nt + (nt - 1 - j), h + off))
    out = jax.ShapeDtypeStruct((t, HG_WIDTH), F32)
    return pl.pallas_call(
        _hgrn_kernel,
        grid=(n_seq, HG_HEADS, nt),
        in_specs=[fwd(0), fwd(0), fwd(0), fwd(0), fwd(0),
                  bwd(0), bwd(HG_HEADS), bwd(HG_HEADS), bwd(HG_HEADS), bwd(0),
                  _const_spec(dall.shape), _const_spec(masks.shape)],
        out_specs=[fwd(0), bwd(0)],
        out_shape=[out, out],
        scratch_shapes=[pltpu.VMEM((2, HG_DK, HG_DV), F32)],
        compiler_params=_cparams(("parallel", "parallel", "arbitrary")),
        name="hgrn2",
    )(q, kk, lfh, lfl, v, q, kk, lfh, lfl, v, dall, masks)


def _s5_matrices(lam_re, lam_im, log_dt, b_re, b_im, c_re, c_im, d_skip):
    hp = lax.Precision.HIGHEST
    ell, g, c, p = S5_L, S5_GROUPS, S5_GROUP, S5_STATE
    no, gpo = S5_OCTETS, S5_GPO
    dt = jnp.exp(log_dt)[..., None]

    def cpow(j):
        j = jnp.asarray(j, F32).reshape((-1, 1, 1, 1))
        mag = jnp.exp(lam_re * dt * j)
        ang = lam_im * dt * j
        return mag * jnp.cos(ang), mag * jnp.sin(ang)

    pr, pi = cpow(np.arange(ell + 1))
    abar_re, abar_im = pr[1], pi[1]
    den = lam_re * lam_re + lam_im * lam_im
    nr = abar_re - 1.0
    ni = abar_im
    cr = ((nr * lam_re + ni * lam_im) / den)[..., None]
    ci = ((ni * lam_re - nr * lam_im) / den)[..., None]
    bb_re = jnp.swapaxes(cr * b_re - ci * b_im, -1, -2)
    bb_im = jnp.swapaxes(cr * b_im + ci * b_re, -1, -2)

    prb, pib = pr[:, :, :, None, :], pi[:, :, :, None, :]
    wr = prb * bb_re - pib * bb_im
    wi = prb * bb_im + pib * bb_re
    kj = jnp.einsum('dgcp,jdgep->jdgce', jnp.concatenate([c_re, -c_im], axis=-1),
                    jnp.concatenate([wr[:ell], wi[:ell]], axis=-1), precision=hp)
    tau = np.arange(ell)

    skip = jnp.eye(c, dtype=F32) * d_skip.reshape(g, c, 1)
    klag = jnp.concatenate([kj[ell - 1:0:-1, 1], (kj[0, 0] + kj[0, 1] + skip)[None], kj[1:ell, 0]], axis=0)
    lag = tau[None, :] - tau[:, None] + (ell - 1)
    tz = klag[lag].transpose(2, 0, 4, 1, 3).reshape(g, S5_CW, S5_CW)

    mf_re, mf_im = wr[ell - 1 - tau, 0], wi[ell - 1 - tau, 0]
    mb_re, mb_im = wr[tau, 1], wi[tau, 1]
    mv = jnp.concatenate([mf_re, mb_re, mf_im, mb_im], axis=-1)
    mv = jnp.swapaxes(mv, 0, 1).reshape(g, S5_CW, S5_GW)

    def cw(powers, d):
        p_re = pr[powers, d][:, :, None, :]
        p_im = pi[powers, d][:, :, None, :]
        return c_re[d] * p_re - c_im[d] * p_im, c_re[d] * p_im + c_im[d] * p_re

    nf_re, nf_im = cw(tau + 1, 0)
    nb_re, nb_im = cw(ell - tau, 1)
    nm = jnp.concatenate([nf_re, nb_re, -nf_im, -nb_im], axis=-1)
    nm = jnp.swapaxes(jnp.swapaxes(nm, 0, 1).reshape(g, S5_CW, S5_GW), 1, 2)

    a_re = jnp.concatenate([pr[ell, 0], pr[ell, 1]], axis=-1).reshape(no, gpo * 2 * p)
    a_im = jnp.concatenate([pi[ell, 0], pi[ell, 1]], axis=-1).reshape(no, gpo * 2 * p)
    a16 = jnp.stack([a_re, a_im], axis=1)
    return tz.astype(BF16), mv.astype(BF16), nm.astype(BF16), a16.astype(F32)


def _s5_kernel(u_ref, mv_ref, a_ref, tz_ref, nm_ref, y_ref, v_scr, *, nseq, nchunk, segmented):
    hw = S5_SW // 2
    ntile = S5_SW // LANES
    gpo = S5_GPO

    def ld(r):
        return jnp.concatenate([v_scr[k, r, :] for k in range(ntile)], axis=1)

    def st(r, val):
        for k in range(ntile):
            v_scr[k, r, :] = val[:, k * LANES:(k + 1) * LANES]

    for i in range(gpo):
        inc = jnp.dot(u_ref[i], mv_ref[i], preferred_element_type=F32)
        for s in range(nseq):
            dst = pl.ds(s, nchunk, stride=nseq)
            v_scr[i, dst, :] = inc[s * nchunk:(s + 1) * nchunk, :LANES]
            v_scr[gpo + i, dst, :] = inc[s * nchunk:(s + 1) * nchunk, LANES:]
    a = a_ref[0]
    ar = jnp.broadcast_to(a[0:1], (nseq, hw))
    ai = jnp.broadcast_to(a[1:2], (nseq, hw))
    lane = lax.broadcasted_iota(jnp.int32, (nseq, S5_SW), 1)
    is_fwd2 = (lane & (2 * S5_STATE - 1)) < S5_STATE
    is_fwd = is_fwd2[:, :hw]

    def rows(i):
        rf = pl.ds(pl.multiple_of(i * nseq, nseq), nseq)
        rb = pl.ds(pl.multiple_of((nchunk - 1 - i) * nseq, nseq), nseq)
        return rf, rb

    def body(i, carry):
        xr, xi = carry
        rf, rb = rows(i)
        vf = ld(rf)
        vb = ld(rb)
        x = jnp.concatenate([xr, xi], axis=1)
        st(rf, jnp.where(is_fwd2, x, vf))
        st(rb, jnp.where(is_fwd2, vb, x))
        vr = jnp.where(is_fwd, vf[:, :hw], vb[:, :hw])
        vi = jnp.where(is_fwd, vf[:, hw:], vb[:, hw:])
        return ar * xr - ai * xi + vr, ar * xi + ai * xr + vi

    zero = jnp.zeros((nseq, hw), F32)
    er, ei = lax.fori_loop(0, nchunk, body, (zero, zero))

    if segmented:
        sr, si = a[0:1], a[1:2]
        for _ in range(nchunk.bit_length() - 1):
            sr, si = sr * sr - si * si, 2.0 * sr * si
        z1 = jnp.zeros((1, hw), F32)
        cf = [(z1, z1)]
        for s in range(nseq - 1):
            pr_, pi_ = cf[-1]
            cf.append((er[s:s + 1] + sr * pr_ - si * pi_, ei[s:s + 1] + sr * pi_ + si * pr_))
        cb = [(z1, z1)]
        for s in range(nseq - 1, 0, -1):
            pr_, pi_ = cb[0]
            cb.insert(0, (er[s:s + 1] + sr * pr_ - si * pi_, ei[s:s + 1] + sr * pi_ + si * pr_))
        cr = jnp.where(is_fwd, jnp.concatenate([t[0] for t in cf], axis=0),
                       jnp.concatenate([t[0] for t in cb], axis=0))
        ci = jnp.where(is_fwd, jnp.concatenate([t[1] for t in cf], axis=0),
                       jnp.concatenate([t[1] for t in cb], axis=0))

        def fix(i, carry):
            cr, ci = carry
            rf, rb = rows(i)
            x = jnp.concatenate([cr, ci], axis=1)
            st(rf, ld(rf) + jnp.where(is_fwd2, x, 0.0))
            st(rb, ld(rb) + jnp.where(is_fwd2, 0.0, x))
            return ar * cr - ai * ci, ar * ci + ai * cr

        lax.fori_loop(0, nchunk, fix, (cr, ci))

    for i in range(gpo):
        xin = []
        for s in range(nseq):
            src = pl.ds(s, nchunk, stride=nseq)
            xin.append(jnp.concatenate([v_scr[i, src, :], v_scr[gpo + i, src, :]], axis=1).astype(BF16))
        y = jnp.dot(u_ref[i], tz_ref[i], preferred_element_type=F32)
        y = y + jnp.dot(jnp.concatenate(xin, axis=0), nm_ref[i], preferred_element_type=F32)
        y_ref[i] = y.astype(BF16)


def _s5(ug, n_seq, seq, mats):
    tz, mv, nm, a16 = mats
    nrow = ug.shape[1]
    nseq = S5_ROWS
    segmented = n_seq == 1
    if segmented:
        n_seq, seq = nseq, seq // nseq
    assert n_seq % nseq == 0 and seq % S5_L == 0
    nchunk = seq // S5_L
    assert nchunk & (nchunk - 1) == 0 and nchunk % 16 == 0
    m = nchunk * nseq
    blk = pl.BlockSpec((S5_GPO, m, S5_CW), lambda o, i: (o, i, 0))
    mat = lambda r, c: pl.BlockSpec((S5_GPO, r, c), lambda o, i: (o, 0, 0))
    return pl.pallas_call(
        functools.partial(_s5_kernel, nseq=nseq, nchunk=nchunk, segmented=segmented),
        grid=(S5_OCTETS, nrow // m),
        in_specs=[blk, mat(S5_CW, S5_GW), pl.BlockSpec((1, 2, S5_SW // 2), lambda o, i: (o, 0, 0)),
                  mat(S5_CW, S5_CW), mat(S5_GW, S5_CW)],
        out_specs=blk,
        out_shape=jax.ShapeDtypeStruct((S5_GROUPS, nrow, S5_CW), BF16),
        scratch_shapes=[pltpu.VMEM((S5_SW // LANES, m, LANES), F32)],
        compiler_params=_cparams(("parallel", "arbitrary")),
        name="s5",
    )(ug, mv, a16, tz, nm)


def _rms(x, g):
    return x * lax.rsqrt(jnp.mean(x * x, axis=-1, keepdims=True) + EPS) * g


def _out_kernel(x_ref, of_ref, ob_ref, g_ref, y2_ref, onorm_ref, gluw_ref, glub_ref,
                merge_ref, wout_ref, n2_ref, w1_ref, w2_ref, nf_ref, out_ref, y_scr):
    o = of_ref[...] + ob_ref[...]
    heads = []
    for h in range(HG_HEADS):
        oh = o[:, h * HG_DV:(h + 1) * HG_DV]
        heads.append(oh * lax.rsqrt(jnp.mean(oh * oh, axis=-1, keepdims=True) + EPS))
    g = g_ref[...].astype(F32)
    ya = jnp.concatenate(heads, axis=1) * onorm_ref[...] * (g * jax.nn.sigmoid(g))

    nrow = y_scr.shape[1] // S5_L
    seg = lax.broadcasted_iota(jnp.int32, (nrow, LANES), 1) // S5_GROUP
    for oc in range(S5_OCTETS):
        for half in range(S5_L // S5_GPO):
            src = [y2_ref[oc * S5_GPO + i, :, half * LANES:(half + 1) * LANES].astype(F32)
                   for i in range(S5_GPO)]
            for t8 in range(S5_GPO):
                y_scr[oc, pl.ds(half * S5_GPO + t8, nrow, stride=S5_L), :] = _regroup(src, t8, seg)
    z = jax.nn.gelu(jnp.concatenate([y_scr[oc] for oc in range(S5_OCTETS)], axis=1))
    gate = jnp.dot(z.astype(BF16), gluw_ref[...], preferred_element_type=F32) + glub_ref[...]
    z = z * jax.nn.sigmoid(gate)
    yb = _rms(z, merge_ref[...])

    ycat = jnp.concatenate([ya, yb], axis=1).astype(BF16)
    x1 = x_ref[...] + jnp.dot(ycat, wout_ref[...], preferred_element_type=F32)
    h2 = _rms(x1, n2_ref[...]).astype(BF16)
    acc = x1
    step = D_FF // 4
    for c in range(4):
        hid = jnp.dot(h2, w1_ref[:, c * step:(c + 1) * step], preferred_element_type=F32)
        hid = jnp.square(jnp.maximum(hid, 0.0)).astype(BF16)
        acc = acc + jnp.dot(hid, w2_ref[c * step:(c + 1) * step, :], preferred_element_type=F32)
    out_ref[...] = _rms(acc, nf_ref[...])


def _outstage(x, o_f, o_b, g, y2, onorm, gluw, glub, merge, wout, n2, w1, w2, nf, tm):
    t = x.shape[0]
    row = lambda w: pl.BlockSpec((tm, w), lambda i: (i, 0))
    vec = lambda w: _const_spec((1, w))
    return pl.pallas_call(
        _out_kernel,
        grid=(t // tm,),
        in_specs=[row(D_MODEL), row(HG_WIDTH), row(HG_WIDTH), row(HG_WIDTH),
                  pl.BlockSpec((S5_GROUPS, tm // S5_L, S5_CW), lambda i: (0, i, 0)),
                  vec(HG_WIDTH), _const_spec((S5_WIDTH, S5_WIDTH)), vec(S5_WIDTH),
                  vec(S5_WIDTH), _const_spec((D_MODEL, D_MODEL)), vec(D_MODEL),
                  _const_spec((D_MODEL, D_FF)), _const_spec((D_FF, D_MODEL)), vec(D_MODEL)],
        out_specs=row(D_MODEL),
        out_shape=jax.ShapeDtypeStruct((t, D_MODEL), F32),
        scratch_shapes=[pltpu.VMEM((S5_OCTETS, tm, LANES), F32)],
        compiler_params=_cparams(("parallel",)),
        name="outstage",
    )(x, o_f, o_b, g, y2, onorm, gluw, glub, merge, wout, n2, w1, w2, nf)


def _trunk(x3, p):
    b, s, dm = x3.shape
    x = x3.reshape(b * s, dm)
    q, lfh, lfl, kk, v, g, u2 = _inproj(x, p["g1"], p["w_in"], p["lb"], ROW_TILE_IN)
    o_f, o_b = _hgrn(q, lfh, lfl, kk, v, b, s, min(HG_TILE, s))
    y2 = _s5(u2, b, s, p["s5"])
    out = _outstage(x, o_f, o_b, g, y2, p["onorm"], p["gluw"], p["glub"], p["merge"],
                    p["wout"], p["n2"], p["w1"], p["w2"], p["nf"], ROW_TILE_OUT)
    return out.reshape(b, s, dm)


def kernel(x_prompt, x_sample, norm1_g, w_in, hgrn_lb, hgrn_onorm_g, s5_lambda_re, s5_lambda_im,
           s5_log_dt, s5_b_re, s5_b_im, s5_c_re, s5_c_im, s5_d, s5_glu_w, s5_glu_b, s5_merge_g,
           w_out, norm2_g, w_ff1, w_ff2, norm_f_g):
    l = 0
    lbs = jnp.cumsum(jax.nn.softmax(hgrn_lb.astype(F32), axis=1), axis=1)[:, l]
    row = lambda a: a.astype(F32).reshape(1, -1)
    f = lambda a: a[l].astype(F32)
    p = {
        "g1": row(norm1_g[l]),
        "w_in": w_in[l].astype(BF16),
        "lb": lbs.reshape(1, 2 * HG_FDIM),
        "s5": _s5_matrices(f(s5_lambda_re), f(s5_lambda_im), f(s5_log_dt), f(s5_b_re), f(s5_b_im),
                           f(s5_c_re), f(s5_c_im), f(s5_d)),
        "onorm": row(hgrn_onorm_g[l]),
        "gluw": s5_glu_w[l].astype(BF16),
        "glub": row(s5_glu_b[l]),
        "merge": row(s5_merge_g[l]),
        "wout": w_out[l].astype(BF16),
        "n2": row(norm2_g[l]),
        "w1": w_ff1[l].astype(BF16),
        "w2": w_ff2[l].astype(BF16),
        "nf": row(norm_f_g),
    }
    return (_trunk(x_prompt, p), _trunk(x_sample, p))
```

```python
import functools

import numpy as np
import jax
import jax.numpy as jnp
from jax import lax
from jax.experimental import pallas as pl
from jax.experimental.pallas import tpu as pltpu

F32 = jnp.float32
BF16 = jnp.bfloat16

D_MODEL = 1024
HG_HEADS = 4
HG_DK = 128
HG_DV = 128
HG_FDIM = HG_HEADS * HG_DK
HG_WIDTH = HG_HEADS * HG_DV
S5_WIDTH = 512
S5_GROUP = 16
S5_GROUPS = S5_WIDTH // S5_GROUP
S5_STATE = 64
D_FF = 4 * D_MODEL
IN_WIDTH = 3 * HG_FDIM + 2 * HG_WIDTH + S5_WIDTH
EPS = 1e-6

LANES = 128
VMEM_LIMIT_BYTES = 56 * 1024 * 1024

ROW_TILE_IN = 512
ROW_TILE_OUT = 512
HG_TILE = 2048

HG_C = 64
HG_LEVELS = (32, 16, 8, 4, 2, 1)
HG_SUB = 8
HG_NBLK = 1

S5_L = 16
S5_OCTETS = S5_WIDTH // LANES
S5_GPO = LANES // S5_GROUP
S5_CW = S5_L * S5_GROUP
S5_GW = 4 * S5_STATE
S5_SW = S5_GW * S5_GPO
S5_ROWS = 8


def _cparams(sem):
    return pltpu.CompilerParams(dimension_semantics=sem, vmem_limit_bytes=VMEM_LIMIT_BYTES)


def _const_spec(shape):
    n = len(shape)
    return pl.BlockSpec(shape, lambda *_: (0,) * n, pipeline_mode=pl.Buffered(1))


def _regroup(srcs, sel, seg):
    out = None
    for k, s in enumerate(srcs):
        shift = ((k - sel) % S5_GPO) * S5_GROUP
        r = s if shift == 0 else pltpu.roll(s, shift, axis=1)
        out = r if out is None else jnp.where(seg == k, r, out)
    return out


def _inproj_kernel(x_ref, g1_ref, w_ref, lb_ref, q_ref, lfh_ref, lfl_ref, kk_ref, v_ref, g_ref, u_ref,
                   u_scr):
    x = x_ref[...]
    ms = jnp.mean(x * x, axis=-1, keepdims=True)
    h = (x * lax.rsqrt(ms + EPS) * g1_ref[...]).astype(BF16)
    o1 = HG_FDIM
    o2 = 3 * HG_FDIM
    o3 = o2 + HG_WIDTH
    o4 = o3 + HG_WIDTH
    proj = lambda lo, hi: jnp.dot(h, w_ref[:, lo:hi], preferred_element_type=F32)
    for d in range(2):
        z = proj(o1 + d * HG_FDIM, o1 + (d + 1) * HG_FDIM)
        cols = slice(d * HG_FDIM, (d + 1) * HG_FDIM)
        lb = lb_ref[:, cols]
        sg = jax.nn.sigmoid(z)
        f = lb + (1.0 - lb) * sg
        lf = jnp.log2(f)
        lfh = lf.astype(BF16)
        lfh_ref[:, cols] = lfh
        lfl_ref[:, cols] = (lf - lfh.astype(F32)).astype(BF16)
        kk_ref[:, cols] = ((1.0 - lb) * (1.0 - sg)).astype(BF16)
    nrow = u_scr.shape[1] // S5_L
    seg = lax.broadcasted_iota(jnp.int32, (nrow, LANES), 1) // S5_GROUP
    pu = proj(o4, IN_WIDTH)
    for o in range(S5_OCTETS):
        u_scr[o] = pu[:, o * LANES:(o + 1) * LANES]
        src = [u_scr[o, pl.ds(tau, nrow, stride=S5_L), :] for tau in range(S5_L)]
        for half in range(S5_L // S5_GPO):
            for i in range(S5_GPO):
                u_ref[o * S5_GPO + i, :, half * LANES:(half + 1) * LANES] = _regroup(
                    src[half * S5_GPO:(half + 1) * S5_GPO], i, seg).astype(BF16)
    q_ref[...] = proj(0, o1).astype(BF16)
    v_ref[...] = proj(o2, o3).astype(BF16)
    g_ref[...] = proj(o3, o4).astype(BF16)


def _inproj(x, g1, w_in, lb, tm):
    t = x.shape[0]
    row = lambda w: pl.BlockSpec((tm, w), lambda i: (i, 0))
    outs = [
        jax.ShapeDtypeStruct((t, HG_FDIM), BF16),
        jax.ShapeDtypeStruct((t, 2 * HG_FDIM), BF16),
        jax.ShapeDtypeStruct((t, 2 * HG_FDIM), BF16),
        jax.ShapeDtypeStruct((t, 2 * HG_FDIM), BF16),
        jax.ShapeDtypeStruct((t, HG_WIDTH), BF16),
        jax.ShapeDtypeStruct((t, HG_WIDTH), BF16),
        jax.ShapeDtypeStruct((S5_GROUPS, t // S5_L, S5_CW), BF16),
    ]
    return pl.pallas_call(
        _inproj_kernel,
        grid=(t // tm,),
        in_specs=[row(D_MODEL), _const_spec((1, D_MODEL)), _const_spec((D_MODEL, IN_WIDTH)),
                  _const_spec((1, 2 * HG_FDIM))],
        out_specs=[row(HG_FDIM), row(2 * HG_FDIM), row(2 * HG_FDIM), row(2 * HG_FDIM),
                   row(HG_WIDTH), row(HG_WIDTH),
                   pl.BlockSpec((S5_GROUPS, tm // S5_L, S5_CW), lambda i: (0, i, 0))],
        out_shape=outs,
        scratch_shapes=[pltpu.VMEM((S5_OCTETS, tm, LANES), F32)],
        compiler_params=_cparams(("parallel",)),
        name="inproj",
    )(x, g1, w_in, lb)


def _hgrn_constants():
    c = HG_C
    idx = np.arange(c)
    blocks = [(idx[None, :] <= idx[:, None])]
    masks = []
    for m in HG_LEVELS:
        rho = (idx // (2 * m)) * (2 * m) + m
        late = idx >= rho
        x = idx[None, :]
        r = idx[:, None]
        d = np.where(late[:, None], (x >= rho[:, None]) & (x <= r), (x > r) & (x < rho[:, None]))
        if False:
            blocks.append(d)
        same = (idx[:, None] // (2 * m)) == (idx[None, :] // (2 * m))
        masks.append(same & late[:, None] & (~late)[None, :])
    d_fwd = np.concatenate(blocks, axis=0).astype(np.float32)
    d_bwd = np.concatenate([b[::-1, ::-1] for b in blocks], axis=0).astype(np.float32)
    m_fwd = np.stack(masks).astype(np.float32)
    m_bwd = m_fwd[:, ::-1, ::-1]
    dall = np.stack([d_fwd, d_bwd])
    dall = np.concatenate([dall, dall], axis=-1)
    return dall, np.stack([m_fwd, m_bwd])


def _hgrn_group(q, k, lfh, lfl, v, st_ref, dall, mask_ref, reverse):
    c = HG_C
    nc = q.shape[0] // c
    nt = (((1,), (1,)), ((), ()))
    tn = (((0,), (0,)), ((), ()))
    sl = lambda a, j: a[j * c:(j + 1) * c]
    lf2 = jnp.concatenate(
        [jnp.concatenate([sl(lfh, j), sl(lfl, j)], axis=0) for j in range(nc)], axis=1)
    e_all = jnp.dot(dall, lf2, preferred_element_type=F32)
    row = lax.broadcasted_iota(jnp.int32, (c, HG_DK), 0)
    if reverse:
        row = (c - 1) - row
    sub = HG_SUB
    ones = jnp.ones((HG_DK, c), BF16)
    eye = lax.broadcasted_iota(jnp.int32, (c, c), 0) == lax.broadcasted_iota(jnp.int32, (c, c), 1)
    dg_all = jnp.dot(q * k, ones, preferred_element_type=F32)
    parts = []
    for j in range(nc):
        ej = e_all[:, j * HG_DK:(j + 1) * HG_DK]
        bc = ej[0:c]
        qf = sl(q, j).astype(F32)
        kf = sl(k, j).astype(F32)
        vj = sl(v, j)
        att = [jnp.zeros((sub, c), F32) for _ in range(c // sub)]
        for li, m in enumerate(HG_LEVELS):
            msk = mask_ref[li]
            if m >= sub:
                pieces = []
                for pb in range(c // (2 * m)):
                    lo, mid, hi = pb * 2 * m, pb * 2 * m + m, (pb + 1) * 2 * m
                    ref = bc[mid:mid + 1] if reverse else bc[mid - 1:mid]
                    first, second = bc[lo:mid], bc[mid:hi]
                    pieces += ([first - ref, ref - second] if reverse else [ref - first, second - ref])
                g_l = jnp.exp2(jnp.concatenate(pieces, axis=0))
                late = [b for b in range(c // m) if ((c // m - 1 - b) if reverse else b) % 2 == 1]
                qk = jnp.concatenate(
                    [(qf if b in late else kf)[b * m:(b + 1) * m] for b in range(c // m)], axis=0)
                xl = (qk * g_l).astype(BF16)
                xq = jnp.concatenate([xl[b * m:(b + 1) * m] for b in late], axis=0)
                a_l = lax.dot_general(xq, xl, nt, preferred_element_type=F32)
                for n, b in enumerate(late):
                    for r in range(m // sub):
                        t = b * (m // sub) + r
                        lo = n * m + r * sub
                        att[t] = att[t] + a_l[lo:lo + sub] * msk[t * sub:(t + 1) * sub]
            else:
                late_row = (row & (2 * m - 1)) >= m
                if m == 1:
                    g_l = jnp.exp2(jnp.where(late_row, sl(lfh, j).astype(F32) + sl(lfl, j).astype(F32), 0.0))
                elif m == 2:
                    lfj = sl(lfh, j).astype(F32) + sl(lfl, j).astype(F32)
                    r4 = row & 3
                    if reverse:
                        nxt = pltpu.roll(lfj, 1, axis=0)
                        prv = pltpu.roll(lfj, c - 1, axis=0)
                    else:
                        nxt = pltpu.roll(lfj, c - 1, axis=0)
                        prv = pltpu.roll(lfj, 1, axis=0)
                    e2 = jnp.where(r4 == 0, nxt, jnp.where(r4 == 1, 0.0, jnp.where(r4 == 2, lfj, lfj + prv)))
                    g_l = jnp.exp2(e2)
                else:
                    b3 = bc.reshape(c // sub, sub, HG_DK)
                    rr = m if reverse else m - 1
                    ref = jnp.broadcast_to(b3[:, rr:rr + 1, :], b3.shape).reshape(c, HG_DK)
                    g_l = jnp.exp2(jnp.where(late_row, bc - ref, ref - bc))
                xl = (jnp.where(late_row, qf, kf) * g_l).astype(BF16)
                a_l = lax.dot_general(xl, xl, nt, preferred_element_type=F32)
                for t in range(c // sub):
                    att[t] = att[t] + a_l[t * sub:(t + 1) * sub] * msk[t * sub:(t + 1) * sub]
        att = jnp.concatenate(att, axis=0) + jnp.where(eye, sl(dg_all, j), 0.0)
        intra = jnp.dot(att.astype(BF16), vj, preferred_element_type=F32)
        g_cum = jnp.exp2(bc)
        qt = (qf * g_cum).astype(BF16)
        tot = bc[0:1] if reverse else bc[c - 1:c]
        kh = (kf * jnp.exp2(tot - bc)).astype(BF16)
        upd = lax.dot_general(kh, vj, tn, preferred_element_type=F32)
        g_tot = g_cum[0:1] if reverse else g_cum[c - 1:c]
        parts.append((intra, qt, g_tot, upd))
    st = st_ref[...]
    outs = [None] * nc
    for j in (range(nc - 1, -1, -1) if reverse else range(nc)):
        intra, qt, g_tot, upd = parts[j]
        outs[j] = intra + jnp.dot(qt, st.astype(BF16), preferred_element_type=F32)
        st = st * jnp.transpose(jnp.broadcast_to(g_tot, (HG_DV, HG_DK))) + upd
    st_ref[...] = st
    return jnp.concatenate(outs, axis=0)


def _hgrn_kernel(qf_ref, kf_ref, hf_ref, lf_ref, vf_ref, qb_ref, kb_ref, hb_ref, lb_ref, vb_ref,
                 dall_ref, mask_ref, of_ref, ob_ref, st_ref):
    @pl.when(pl.program_id(2) == 0)
    def _():
        st_ref[...] = jnp.zeros_like(st_ref)

    of_ref[...] = _hgrn_group(qf_ref[...], kf_ref[...], hf_ref[...], lf_ref[...], vf_ref[...],
                              st_ref.at[0], dall_ref[0], mask_ref.at[0], False)
    ob_ref[...] = _hgrn_group(qb_ref[...], kb_ref[...], hb_ref[...], lb_ref[...], vb_ref[...],
                              st_ref.at[1], dall_ref[1], mask_ref.at[1], True)


def _hgrn(q, lfh, lfl, kk, v, n_seq, seq, ts):
    t = q.shape[0]
    nt = seq // ts
    assert seq % ts == 0 and ts % HG_C == 0
    dall, masks = _hgrn_constants()
    dall = jnp.asarray(dall, BF16)
    masks = jnp.asarray(masks, F32)
    fwd = lambda off: pl.BlockSpec((ts, HG_DK), lambda b, h, j: (b * nt + j, h + off))
    bwd = lambda off: pl.BlockSpec((ts, HG_DK), lambda b, h, j: (b * nt + (nt - 1 - j), h + off))
    out = jax.ShapeDtypeStruct((t, HG_WIDTH), F32)
    return pl.pallas_call(
        _hgrn_kernel,
        grid=(n_seq, HG_HEADS, nt),
        in_specs=[fwd(0), fwd(0), fwd(0), fwd(0), fwd(0),
                  bwd(0), bwd(HG_HEADS), bwd(HG_HEADS), bwd(HG_HEADS), bwd(0),
                  _const_spec(dall.shape), _const_spec(masks.shape)],
        out_specs=[fwd(0), bwd(0)],
        out_shape=[out, out],
        scratch_shapes=[pltpu.VMEM((2, HG_DK, HG_DV), F32)],
        compiler_params=_cparams(("parallel", "parallel", "arbitrary")),
        name="hgrn2",
    )(q, kk, lfh, lfl, v, q, kk, lfh, lfl, v, dall, masks)


def _s5_matrices(lam_re, lam_im, log_dt, b_re, b_im, c_re, c_im, d_skip):
    hp = lax.Precision.HIGHEST
    ell, g, c, p = S5_L, S5_GROUPS, S5_GROUP, S5_STATE
    no, gpo = S5_OCTETS, S5_GPO
    dt = jnp.exp(log_dt)[..., None]

    def cpow(j):
        j = jnp.asarray(j, F32).reshape((-1, 1, 1, 1))
        mag = jnp.exp(lam_re * dt * j)
        ang = lam_im * dt * j
        return mag * jnp.cos(ang), mag * jnp.sin(ang)

    pr, pi = cpow(np.arange(ell + 1))
    abar_re, abar_im = pr[1], pi[1]
    den = lam_re * lam_re + lam_im * lam_im
    nr = abar_re - 1.0
    ni = abar_im
    cr = ((nr * lam_re + ni * lam_im) / den)[..., None]
    ci = ((ni * lam_re - nr * lam_im) / den)[..., None]
    bb_re = jnp.swapaxes(cr * b_re - ci * b_im, -1, -2)
    bb_im = jnp.swapaxes(cr * b_im + ci * b_re, -1, -2)

    prb, pib = pr[:, :, :, None, :], pi[:, :, :, None, :]
    wr = prb * bb_re - pib * bb_im
    wi = prb * bb_im + pib * bb_re
    kj = jnp.einsum('dgcp,jdgep->jdgce', jnp.concatenate([c_re, -c_im], axis=-1),
                    jnp.concatenate([wr[:ell], wi[:ell]], axis=-1), precision=hp)
    tau = np.arange(ell)

    skip = jnp.eye(c, dtype=F32) * d_skip.reshape(g, c, 1)
    klag = jnp.concatenate([kj[ell - 1:0:-1, 1], (kj[0, 0] + kj[0, 1] + skip)[None], kj[1:ell, 0]], axis=0)
    lag = tau[None, :] - tau[:, None] + (ell - 1)
    tz = klag[lag].transpose(2, 0, 4, 1, 3).reshape(g, S5_CW, S5_CW)

    mf_re, mf_im = wr[ell - 1 - tau, 0], wi[ell - 1 - tau, 0]
    mb_re, mb_im = wr[tau, 1], wi[tau, 1]
    mv = jnp.concatenate([mf_re, mb_re, mf_im, mb_im], axis=-1)
    mv = jnp.swapaxes(mv, 0, 1).reshape(g, S5_CW, S5_GW)

    def cw(powers, d):
        p_re = pr[powers, d][:, :, None, :]
        p_im = pi[powers, d][:, :, None, :]
        return c_re[d] * p_re - c_im[d] * p_im, c_re[d] * p_im + c_im[d] * p_re

    nf_re, nf_im = cw(tau + 1, 0)
    nb_re, nb_im = cw(ell - tau, 1)
    nm = jnp.concatenate([nf_re, nb_re, -nf_im, -nb_im], axis=-1)
    nm = jnp.swapaxes(jnp.swapaxes(nm, 0, 1).reshape(g, S5_CW, S5_GW), 1, 2)

    a_re = jnp.concatenate([pr[ell, 0], pr[ell, 1]], axis=-1).reshape(no, gpo * 2 * p)
    a_im = jnp.concatenate([pi[ell, 0], pi[ell, 1]], axis=-1).reshape(no, gpo * 2 * p)
    a16 = jnp.stack([a_re, a_im], axis=1)
    return tz.astype(BF16), mv.astype(BF16), nm.astype(BF16), a16.astype(F32)


def _s5_kernel(u_ref, mv_ref, a_ref, tz_ref, nm_ref, y_ref, v_scr, *, nseq, nchunk, segmented):
    hw = S5_SW // 2
    ntile = S5_SW // LANES
    gpo = S5_GPO

    def ld(r):
        return jnp.concatenate([v_scr[k, r, :] for k in range(ntile)], axis=1)

    def st(r, val):
        for k in range(ntile):
            v_scr[k, r, :] = val[:, k * LANES:(k + 1) * LANES]

    for i in range(gpo):
        inc = jnp.dot(u_ref[i], mv_ref[i], preferred_element_type=F32)
        for s in range(nseq):
            dst = pl.ds(s, nchunk, stride=nseq)
            v_scr[i, dst, :] = inc[s * nchunk:(s + 1) * nchunk, :LANES]
            v_scr[gpo + i, dst, :] = inc[s * nchunk:(s + 1) * nchunk, LANES:]
    a = a_ref[0]
    ar = jnp.broadcast_to(a[0:1], (nseq, hw))
    ai = jnp.broadcast_to(a[1:2], (nseq, hw))
    lane = lax.broadcasted_iota(jnp.int32, (nseq, S5_SW), 1)
    is_fwd2 = (lane & (2 * S5_STATE - 1)) < S5_STATE
    is_fwd = is_fwd2[:, :hw]

    def rows(i):
        rf = pl.ds(pl.multiple_of(i * nseq, nseq), nseq)
        rb = pl.ds(pl.multiple_of((nchunk - 1 - i) * nseq, nseq), nseq)
        return rf, rb

    def body(i, carry):
        xr, xi = carry
        rf, rb = rows(i)
        vf = ld(rf)
        vb = ld(rb)
        x = jnp.concatenate([xr, xi], axis=1)
        st(rf, jnp.where(is_fwd2, x, vf))
        st(rb, jnp.where(is_fwd2, vb, x))
        vr = jnp.where(is_fwd, vf[:, :hw], vb[:, :hw])
        vi = jnp.where(is_fwd, vf[:, hw:], vb[:, hw:])
        return ar * xr - ai * xi + vr, ar * xi + ai * xr + vi

    zero = jnp.zeros((nseq, hw), F32)
    er, ei = lax.fori_loop(0, nchunk, body, (zero, zero))

    if segmented:
        sr, si = a[0:1], a[1:2]
        for _ in range(nchunk.bit_length() - 1):
            sr, si = sr * sr - si * si, 2.0 * sr * si
        z1 = jnp.zeros((1, hw), F32)
        cf = [(z1, z1)]
        for s in range(nseq - 1):
            pr_, pi_ = cf[-1]
            cf.append((er[s:s + 1] + sr * pr_ - si * pi_, ei[s:s + 1] + sr * pi_ + si * pr_))
        cb = [(z1, z1)]
        for s in range(nseq - 1, 0, -1):
            pr_, pi_ = cb[0]
            cb.insert(0, (er[s:s + 1] + sr * pr_ - si * pi_, ei[s:s + 1] + sr * pi_ + si * pr_))
        cr = jnp.where(is_fwd, jnp.concatenate([t[0] for t in cf], axis=0),
                       jnp.concatenate([t[0] for t in cb], axis=0))
        ci = jnp.where(is_fwd, jnp.concatenate([t[1] for t in cf], axis=0),
                       jnp.concatenate([t[1] for t in cb], axis=0))

        def fix(i, carry):
            cr, ci = carry
            rf, rb = rows(i)
            x = jnp.concatenate([cr, ci], axis=1)
            st(rf, ld(rf) + jnp.where(is_fwd2, x, 0.0))
            st(rb, ld(rb) + jnp.where(is_fwd2, 0.0, x))
            return ar * cr - ai * ci, ar * ci + ai * cr

        lax.fori_loop(0, nchunk, fix, (cr, ci))

    for i in range(gpo):
        xin = []
        for s in range(nseq):
            src = pl.ds(s, nchunk, stride=nseq)
            xin.append(jnp.concatenate([v_scr[i, src, :], v_scr[gpo + i, src, :]], axis=1).astype(BF16))
        y = jnp.dot(u_ref[i], tz_ref[i], preferred_element_type=F32)
        y = y + jnp.dot(jnp.concatenate(xin, axis=0), nm_ref[i], preferred_element_type=F32)
        y_ref[i] = y.astype(BF16)


def _s5(ug, n_seq, seq, mats):
    tz, mv, nm, a16 = mats
    nrow = ug.shape[1]
    nseq = S5_ROWS
    segmented = n_seq == 1
    if segmented:
        n_seq, seq = nseq, seq // nseq
    assert n_seq % nseq == 0 and seq % S5_L == 0
    nchunk = seq // S5_L
    assert nchunk & (nchunk - 1) == 0 and nchunk % 16 == 0
    m = nchunk * nseq
    blk = pl.BlockSpec((S5_GPO, m, S5_CW), lambda o, i: (o, i, 0))
    mat = lambda r, c: pl.BlockSpec((S5_GPO, r, c), lambda o, i: (o, 0, 0))
    return pl.pallas_call(
        functools.partial(_s5_kernel, nseq=nseq, nchunk=nchunk, segmented=segmented),
        grid=(S5_OCTETS, nrow // m),
        in_specs=[blk, mat(S5_CW, S5_GW), pl.BlockSpec((1, 2, S5_SW // 2), lambda o, i: (o, 0, 0)),
                  mat(S5_CW, S5_CW), mat(S5_GW, S5_CW)],
        out_specs=blk,
        out_shape=jax.ShapeDtypeStruct((S5_GROUPS, nrow, S5_CW), BF16),
        scratch_shapes=[pltpu.VMEM((S5_SW // LANES, m, LANES), F32)],
        compiler_params=_cparams(("parallel", "arbitrary")),
        name="s5",
    )(ug, mv, a16, tz, nm)


def _rms(x, g):
    return x * lax.rsqrt(jnp.mean(x * x, axis=-1, keepdims=True) + EPS) * g


def _out_kernel(x_ref, of_ref, ob_ref, g_ref, y2_ref, onorm_ref, gluw_ref, glub_ref,
                merge_ref, wout_ref, n2_ref, w1_ref, w2_ref, nf_ref, out_ref, y_scr):
    o = of_ref[...] + ob_ref[...]
    heads = []
    for h in range(HG_HEADS):
        oh = o[:, h * HG_DV:(h + 1) * HG_DV]
        heads.append(oh * lax.rsqrt(jnp.mean(oh * oh, axis=-1, keepdims=True) + EPS))
    g = g_ref[...].astype(F32)
    ya = jnp.concatenate(heads, axis=1) * onorm_ref[...] * (g * jax.nn.sigmoid(g))

    nrow = y_scr.shape[1] // S5_L
    seg = lax.broadcasted_iota(jnp.int32, (nrow, LANES), 1) // S5_GROUP
    for oc in range(S5_OCTETS):
        for half in range(S5_L // S5_GPO):
            src = [y2_ref[oc * S5_GPO + i, :, half * LANES:(half + 1) * LANES].astype(F32)
                   for i in range(S5_GPO)]
            for t8 in range(S5_GPO):
                y_scr[oc, pl.ds(half * S5_GPO + t8, nrow, stride=S5_L), :] = _regroup(src, t8, seg)
    z = jax.nn.gelu(jnp.concatenate([y_scr[oc] for oc in range(S5_OCTETS)], axis=1))
    gate = jnp.dot(z.astype(BF16), gluw_ref[...], preferred_element_type=F32) + glub_ref[...]
    z = z * jax.nn.sigmoid(gate)
    yb = _rms(z, merge_ref[...])

    ycat = jnp.concatenate([ya, yb], axis=1).astype(BF16)
    x1 = x_ref[...] + jnp.dot(ycat, wout_ref[...], preferred_element_type=F32)
    h2 = _rms(x1, n2_ref[...]).astype(BF16)
    acc = x1
    step = D_FF // 4
    for c in range(4):
        hid = jnp.dot(h2, w1_ref[:, c * step:(c + 1) * step], preferred_element_type=F32)
        hid = jnp.square(jnp.maximum(hid, 0.0)).astype(BF16)
        acc = acc + jnp.dot(hid, w2_ref[c * step:(c + 1) * step, :], preferred_element_type=F32)
    out_ref[...] = _rms(acc, nf_ref[...])


def _outstage(x, o_f, o_b, g, y2, onorm, gluw, glub, merge, wout, n2, w1, w2, nf, tm):
    t = x.shape[0]
    row = lambda w: pl.BlockSpec((tm, w), lambda i: (i, 0))
    vec = lambda w: _const_spec((1, w))
    return pl.pallas_call(
        _out_kernel,
        grid=(t // tm,),
        in_specs=[row(D_MODEL), row(HG_WIDTH), row(HG_WIDTH), row(HG_WIDTH),
                  pl.BlockSpec((S5_GROUPS, tm // S5_L, S5_CW), lambda i: (0, i, 0)),
                  vec(HG_WIDTH), _const_spec((S5_WIDTH, S5_WIDTH)), vec(S5_WIDTH),
                  vec(S5_WIDTH), _const_spec((D_MODEL, D_MODEL)), vec(D_MODEL),
                  _const_spec((D_MODEL, D_FF)), _const_spec((D_FF, D_MODEL)), vec(D_MODEL)],
        out_specs=row(D_MODEL),
        out_shape=jax.ShapeDtypeStruct((t, D_MODEL), F32),
        scratch_shapes=[pltpu.VMEM((S5_OCTETS, tm, LANES), F32)],
        compiler_params=_cparams(("parallel",)),
        name="outstage",
    )(x, o_f, o_b, g, y2, onorm, gluw, glub, merge, wout, n2, w1, w2, nf)


def _trunk(x3, p):
    b, s, dm = x3.shape
    x = x3.reshape(b * s, dm)
    q, lfh, lfl, kk, v, g, u2 = _inproj(x, p["g1"], p["w_in"], p["lb"], ROW_TILE_IN)
    o_f, o_b = _hgrn(q, lfh, lfl, kk, v, b, s, min(HG_TILE, s))
    y2 = _s5(u2, b, s, p["s5"])
    out = _outstage(x, o_f, o_b, g, y2, p["onorm"], p["gluw"], p["glub"], p["merge"],
                    p["wout"], p["n2"], p["w1"], p["w2"], p["nf"], ROW_TILE_OUT)
    return out.reshape(b, s, dm)


def kernel(x_prompt, x_sample, norm1_g, w_in, hgrn_lb, hgrn_onorm_g, s5_lambda_re, s5_lambda_im,
           s5_log_dt, s5_b_re, s5_b_im, s5_c_re, s5_c_im, s5_d, s5_glu_w, s5_glu_b, s5_merge_g,
           w_out, norm2_g, w_ff1, w_ff2, norm_f_g):
    l = 0
    lbs = jnp.cumsum(jax.nn.softmax(hgrn_lb.astype(F32), axis=1), axis=1)[:, l]
    row = lambda a: a.astype(F32).reshape(1, -1)
    f = lambda a: a[l].astype(F32)
    p = {
        "g1": row(norm1_g[l]),
        "w_in": w_in[l].astype(BF16),
        "lb": lbs.reshape(1, 2 * HG_FDIM),
        "s5": _s5_matrices(f(s5_lambda_re), f(s5_lambda_im), f(s5_log_dt), f(s5_b_re), f(s5_b_im),
                           f(s5_c_re), f(s5_c_im), f(s5_d)),
        "onorm": row(hgrn_onorm_g[l]),
        "gluw": s5_glu_w[l].astype(BF16),
        "glub": row(s5_glu_b[l]),
        "merge": row(s5_merge_g[l]),
        "wout": w_out[l].astype(BF16),
        "n2": row(norm2_g[l]),
        "w1": w_ff1[l].astype(BF16),
        "w2": w_ff2[l].astype(BF16),
        "nf": row(norm_f_g),
    }
    return (_trunk(x_prompt, p), _trunk(x_sample, p))
```
